```python
import math
import jax
import jax.numpy as jnp
from jax import lax
import numpy as np

D_MODEL = 1024
BATCH = 8
SEQ = 4096
DEPTH = 2

CTX_LEN = 256
GRID_W = 64
EPS = 1e-6

MLA_HEADS = D_MODEL // 128
MLA_NOPE = 64
MLA_ROPE = 32
MLA_V = 64
MLA_Q_RANK = 3 * D_MODEL // 8
MLA_KV_RANK = D_MODEL // 4
MLA_WIDTH = MLA_HEADS * MLA_V
ROPE_THETA = 10000.0
Q_BLOCK = 128

ML_HEADS = 4
ML_HEAD_DIM = D_MODEL // (4 * ML_HEADS)
ML_WIDTH = ML_HEADS * ML_HEAD_DIM
ML_CHUNK = 64
ML_CONV = 3

HY_WIDTH = D_MODEL // 4
HY_CONV = 3
HY_BANDS = 16
HY_EMB = 1 + 2 * HY_BANDS
HY_HIDDEN = 64
HY_DECAY_TARGET = 1e-2
HY_FAST_PCT = 0.3
HY_SLOW_PCT = 1.5
HY_SHIFT = 0.05

MIX_WIDTH = MLA_WIDTH + ML_WIDTH + HY_WIDTH
N_MLA_IN = MLA_Q_RANK + MLA_KV_RANK + MLA_ROPE
N_ML_IN = 3 * ML_WIDTH + 4 * ML_HEADS
N_HY_IN = 3 * HY_WIDTH
N_IN = N_MLA_IN + N_ML_IN + N_HY_IN

D_FF = ((8 * D_MODEL // 3 + 255) // 256) * 256
FFN_CONV = 3

kernel_name = 'hybrid_mla_mlstm_hyena_dit_block'


def rmsnorm(x, g):
    xf = x.astype(jnp.float32)
    y = xf * lax.rsqrt(jnp.mean(xf * xf, axis=-1, keepdims=True) + EPS)
    return (y * g.astype(jnp.float32)).astype(x.dtype)


def dwconv(x, w, b):
    width = w.shape[0]
    pad = (width - 1) // 2
    seq = x.shape[1]
    xp = jnp.pad(x, ((0, 0), (pad, pad), (0, 0)))
    y = b
    for j in range(width):
        y = y + xp[:, j:j + seq] * w[j]
    return y


def grid_angles(rows):
    n_freq = MLA_ROPE // 4
    inv = ROPE_THETA ** (-jnp.arange(n_freq, dtype=jnp.float32) / n_freq)
    row = jnp.repeat(jnp.arange(rows, dtype=jnp.float32), GRID_W)
    col = jnp.tile(jnp.arange(GRID_W, dtype=jnp.float32), rows)
    return jnp.concatenate([row[:, None] * inv, col[:, None] * inv], axis=-1)


def axial_rope(x, ang):
    half = x.shape[-1] // 2
    shape = (ang.shape[0],) + (1,) * (x.ndim - 3) + (half,)
    cos = jnp.cos(ang).reshape(shape).astype(x.dtype)
    sin = jnp.sin(ang).reshape(shape).astype(x.dtype)
    x1, x2 = x[..., :half], x[..., half:]
    return jnp.concatenate([x1 * cos - x2 * sin, x1 * sin + x2 * cos], axis=-1)


def mla_kv(p_kv, kv_norm_g, w_ukv):
    b, seq = p_kv.shape[:2]
    kv = (rmsnorm(p_kv[..., :MLA_KV_RANK], kv_norm_g) @ w_ukv).reshape(b, seq, MLA_HEADS, MLA_NOPE + MLA_V)
    return kv[..., :MLA_NOPE], p_kv[..., MLA_KV_RANK:], kv[..., MLA_NOPE:]


def mla_q(p_q, q_norm_g, w_uq):
    b, seq = p_q.shape[:2]
    q = (rmsnorm(p_q, q_norm_g) @ w_uq).reshape(b, seq, MLA_HEADS, MLA_NOPE + MLA_ROPE)
    return q[..., :MLA_NOPE], q[..., MLA_NOPE:]


def block_attention(q_nope, q_rope, k_nope, k_rope, v):
    b, n_q, h, _ = q_nope.shape
    n_blk = n_q // Q_BLOCK
    scale = (MLA_NOPE + MLA_ROPE) ** -0.5

    def blocks(a):
        return jnp.moveaxis(a.reshape((b, n_blk, Q_BLOCK) + a.shape[2:]), 1, 0)

    def one_block(args):
        qn, qr = args
        s = jnp.einsum('bqhd,bkhd->bhqk', qn, k_nope) + jnp.einsum('bqhr,bkr->bhqk', qr, k_rope)
        p = jax.nn.softmax(s.astype(jnp.float32) * scale, axis=-1).astype(v.dtype)
        return jnp.einsum('bhqk,bkhd->bqhd', p, v)

    o = lax.map(one_block, (blocks(q_nope), blocks(q_rope)))
    return jnp.moveaxis(o, 0, 1).reshape(b, n_q, h * v.shape[-1])


def mlstm_features(p, conv_w, conv_b, wq, wk, gate_b):
    b, seq = p.shape[:2]
    w = ML_WIDTH
    u = jax.nn.silu(dwconv(p[..., :w], conv_w, conv_b)).reshape(b, seq, ML_HEADS, ML_HEAD_DIM)
    q = jnp.einsum('blhd,hde->blhe', u, wq)
    k = jnp.einsum('blhd,hde->blhe', u, wk) * (ML_HEAD_DIM ** -0.5)
    v = p[..., w:2 * w].reshape(b, seq, ML_HEADS, ML_HEAD_DIM)
    o = p[..., 2 * w:3 * w].reshape(b, seq, ML_HEADS, ML_HEAD_DIM)
    gates = p[..., 3 * w:].reshape(b, seq, 4, ML_HEADS).astype(jnp.float32) + gate_b.astype(jnp.float32)
    return q, k, v, o, gates


def mlstm_zero_state(b):
    f32 = jnp.float32
    return (jnp.zeros((b, ML_HEADS, ML_HEAD_DIM, ML_HEAD_DIM), f32),
            jnp.zeros((b, ML_HEADS, ML_HEAD_DIM), f32),
            jnp.zeros((b, ML_HEADS), f32))


def mlstm_chunk_scan(q, k, v, logi, logf, state):
    b, seq, h, dh = q.shape
    n_chunk = seq // ML_CHUNK
    f32 = jnp.float32
    causal = jnp.tril(jnp.ones((ML_CHUNK, ML_CHUNK), bool))

    def chunks(a):
        return jnp.moveaxis(a.astype(f32).reshape((b, n_chunk, ML_CHUNK) + a.shape[2:]), 1, 0)

    def step(carry, xs):
        c_mat, n_vec, m = carry
        qc, kc, vc, li, lf = xs
        cum = jnp.cumsum(lf, axis=1).transpose(0, 2, 1)
        li = li.transpose(0, 2, 1)
        dlog = jnp.where(causal, cum[..., :, None] - cum[..., None, :] + li[..., None, :], -jnp.inf)
        inter = cum + m[..., None]
        m_out = jnp.maximum(inter, jnp.max(dlog, axis=-1))
        s = jnp.einsum('bthd,bshd->bhts', qc, kc) * jnp.exp(dlog - m_out[..., None])
        a = jnp.exp(inter - m_out)
        num = jnp.einsum('bhts,bshd->bthd', s, vc) + jnp.einsum('bht,bthd,bhde->bthe', a, qc, c_mat)
        den = jnp.sum(s, axis=-1) + a * jnp.einsum('bthd,bhd->bht', qc, n_vec)
        out = num / jnp.maximum(jnp.abs(den), jnp.exp(-m_out)).transpose(0, 2, 1)[..., None]
        cum_last = cum[..., -1]
        g = cum_last[..., None] - cum + li
        m_new = jnp.maximum(cum_last + m, jnp.max(g, axis=-1))
        wts = jnp.exp(g - m_new[..., None])
        decay = jnp.exp(cum_last + m - m_new)
        c_mat = decay[..., None, None] * c_mat + jnp.einsum('bhs,bshd,bshe->bhde', wts, kc, vc)
        n_vec = decay[..., None] * n_vec + jnp.einsum('bhs,bshd->bhd', wts, kc)
        return (c_mat, n_vec, m_new), out

    state, out = lax.scan(step, state, tuple(chunks(a) for a in (q, k, v, logi, logf)))
    out = jnp.moveaxis(out, 0, 1).reshape(b, seq, h, dh)
    return out.astype(v.dtype), state


def mlstm_direction(feat, direction, state):
    q, k, v, _, gates = feat
    logi = gates[:, :, 2 * direction]
    logf = jax.nn.log_sigmoid(gates[:, :, 2 * direction + 1])
    seqs = (q, k, v, logi, logf)
    if direction == 1:
        seqs = tuple(jnp.flip(a, axis=1) for a in seqs)
    out, state = mlstm_chunk_scan(*seqs, state)
    if direction == 1:
        out = jnp.flip(out, axis=1)
    return out, state


def mlstm_output(h, o, norm_g):
    b, seq = h.shape[:2]
    h = rmsnorm(h * jax.nn.sigmoid(o), norm_g.reshape(ML_HEADS, ML_HEAD_DIM))
    return h.reshape(b, seq, ML_WIDTH)


def mlstm_mixer(px, pc, conv_w, conv_b, wq, wk, gate_b, norm_g, need_ctx):
    fx = mlstm_features(px, conv_w, conv_b, wq, wk, gate_b)
    fc = mlstm_features(pc, conv_w, conv_b, wq, wk, gate_b)
    zero = mlstm_zero_state(px.shape[0])
    hx, hc = [], []
    for direction in (0, 1):
        h_c, ctx_state = mlstm_direction(fc, direction, zero)
        h_x, _ = mlstm_direction(fx, direction, ctx_state)
        hx.append(h_x)
        hc.append(h_c)
    out_x = mlstm_output(hx[0] + hx[1], fx[3], norm_g)
    out_c = mlstm_output(hc[0] + hc[1], fc[3], norm_g) if need_ctx else None
    return out_x, out_c


def hyena_filter(seq, w1, b1, w2, b2, w3, sin_freq):
    f32 = jnp.float32
    t = jnp.linspace(0.0, 1.0, seq, dtype=f32)[:, None]
    omega = 2.0 * math.pi * jnp.arange(seq, dtype=f32) / seq
    bands = jnp.linspace(1e-4, HY_BANDS - 1, HY_BANDS, dtype=f32)
    ang = omega[:, None] * bands[None, :]
    z = jnp.concatenate([t, jnp.cos(ang), -jnp.sin(ang)], axis=-1)
    freq = sin_freq.astype(f32)
    hdn = jnp.sin(freq * (z @ w1.astype(f32) + b1.astype(f32)))
    hdn = jnp.sin(freq * (hdn @ w2.astype(f32) + b2.astype(f32)))
    filt = hdn @ w3.astype(f32)
    deltas = jnp.abs(jnp.linspace(math.log(HY_DECAY_TARGET) / HY_SLOW_PCT,
                                  math.log(HY_DECAY_TARGET) / HY_FAST_PCT, HY_WIDTH, dtype=f32))
    window = jnp.exp(-t * deltas) + HY_SHIFT
    h_fwd = filt[:, :HY_WIDTH] * window
    h_bwd = filt[:, HY_WIDTH:] * window
    l1 = jnp.sum(jnp.abs(h_fwd), axis=0) + jnp.sum(jnp.abs(h_bwd[1:]), axis=0)
    taps = jnp.concatenate([h_fwd, jnp.zeros((1, HY_WIDTH), f32), h_bwd[:0:-1]], axis=0)
    return taps / l1


def hyena_mixer(p, conv_w, conv_b, w1, b1, w2, b2, w3, sin_freq, bias_d):
    seq = p.shape[1]
    u = dwconv(p, conv_w, conv_b)
    x0, x1, v = u[..., :HY_WIDTH], u[..., HY_WIDTH:2 * HY_WIDTH], u[..., 2 * HY_WIDTH:]
    z = (x1 * v).astype(jnp.float32)
    taps = hyena_filter(seq, w1, b1, w2, b2, w3, sin_freq)
    n_fft = 2 * seq
    y = jnp.fft.irfft(jnp.fft.rfft(z, n=n_fft, axis=1) * jnp.fft.rfft(taps, n=n_fft, axis=0)[None],
                      n=n_fft, axis=1)[:, :seq]
    y = y + z * bias_d.astype(jnp.float32)
    return x0 * y.astype(p.dtype)


def conv_ffn(h, w_up, conv_w, conv_b, w_down):
    u = dwconv(h @ w_up, conv_w, conv_b)
    return (jax.nn.silu(u[..., :D_FF]) * u[..., D_FF:]) @ w_down


def setup_inputs(seed: int = 0) -> dict:
    key = jax.random.key(seed)
    ks = iter(jax.random.split(key, 48))

    def nrm(shape, scale):
        return jax.random.normal(next(ks), shape, jnp.float32) * scale

    def gain(shape):
        return 1.0 + nrm(shape, 0.05)

    d = D_MODEL
    lin = jnp.linspace(3.0, 6.0, ML_HEADS, dtype=jnp.float32)
    zer = jnp.zeros((ML_HEADS,), jnp.float32)
    gate_base = jnp.stack([zer, lin, zer, lin])
    return {
        'x': nrm((BATCH, SEQ, d), 1.0),
        'c': nrm((BATCH, d), 1.0),
        'ctx': nrm((BATCH, CTX_LEN, d), 1.0),
        'c_ctx': nrm((d,), 1.0),
        'ada_w': nrm((DEPTH, d, 6 * d), 0.5 * d ** -0.5),
        'ada_b': nrm((DEPTH, 6 * d), 0.02),
        'norm1_g': gain((DEPTH, d)),
        'norm2_g': gain((DEPTH, d)),
        'w_in': nrm((DEPTH, d, N_IN), d ** -0.5),
        'mla_q_norm_g': gain((DEPTH, MLA_Q_RANK)),
        'mla_kv_norm_g': gain((DEPTH, MLA_KV_RANK)),
        'mla_w_uq': nrm((DEPTH, MLA_Q_RANK, MLA_HEADS * (MLA_NOPE + MLA_ROPE)), MLA_Q_RANK ** -0.5),
        'mla_w_ukv': nrm((DEPTH, MLA_KV_RANK, MLA_HEADS * (MLA_NOPE + MLA_V)), MLA_KV_RANK ** -0.5),
        'ml_conv_w': nrm((DEPTH, ML_CONV, ML_WIDTH), ML_CONV ** -0.5),
        'ml_conv_b': nrm((DEPTH, ML_WIDTH), 0.02),
        'ml_wq': nrm((DEPTH, ML_HEADS, ML_HEAD_DIM, ML_HEAD_DIM), ML_HEAD_DIM ** -0.5),
        'ml_wk': nrm((DEPTH, ML_HEADS, ML_HEAD_DIM, ML_HEAD_DIM), ML_HEAD_DIM ** -0.5),
        'ml_gate_b': gate_base[None] + nrm((DEPTH, 4, ML_HEADS), 0.1),
        'ml_norm_g': gain((DEPTH, ML_WIDTH)),
        'hy_conv_w': nrm((DEPTH, HY_CONV, N_HY_IN), HY_CONV ** -0.5),
        'hy_conv_b': nrm((DEPTH, N_HY_IN), 0.02),
        'hy_w1': nrm((DEPTH, HY_EMB, HY_HIDDEN), HY_EMB ** -0.5),
        'hy_b1': nrm((DEPTH, HY_HIDDEN), 0.02),
        'hy_w2': nrm((DEPTH, HY_HIDDEN, HY_HIDDEN), HY_HIDDEN ** -0.5),
        'hy_b2': nrm((DEPTH, HY_HIDDEN), 0.02),
        'hy_w3': nrm((DEPTH, HY_HIDDEN, 2 * HY_WIDTH), HY_HIDDEN ** -0.5),
        'hy_sin_freq': gain((DEPTH, HY_HIDDEN)),
        'hy_bias_d': nrm((DEPTH, HY_WIDTH), 0.1),
        'w_out': nrm((DEPTH, MIX_WIDTH, d), MIX_WIDTH ** -0.5),
        'ffn_w_up': nrm((DEPTH, d, 2 * D_FF), d ** -0.5),
        'ffn_conv_w': nrm((DEPTH, FFN_CONV, 2 * D_FF), FFN_CONV ** -0.5),
        'ffn_conv_b': nrm((DEPTH, 2 * D_FF), 0.02),
        'ffn_w_down': nrm((DEPTH, D_FF, d), D_FF ** -0.5),
        'final_norm_g': gain((d,)),
    }


def reference(x, c, ctx, c_ctx, ada_w, ada_b, norm1_g, norm2_g, w_in, mla_q_norm_g, mla_kv_norm_g,
              mla_w_uq, mla_w_ukv, ml_conv_w, ml_conv_b, ml_wq, ml_wk, ml_gate_b, ml_norm_g,
              hy_conv_w, hy_conv_b, hy_w1, hy_b1, hy_w2, hy_b2, hy_w3, hy_sin_freq, hy_bias_d,
              w_out, ffn_w_up, ffn_conv_w, ffn_conv_b, ffn_w_down, final_norm_g):
    n_rows = x.shape[1] // GRID_W
    ang = grid_angles(n_rows)
    ml_lo = N_MLA_IN
    hy_lo = N_MLA_IN + N_ML_IN
    for i in range(DEPTH):
        last = i == DEPTH - 1
        mod_x = jax.nn.silu(c) @ ada_w[i] + ada_b[i]
        mod_c = jax.nn.silu(c_ctx) @ ada_w[i] + ada_b[i]
        sh1, sc1, g1, sh2, sc2, g2 = jnp.split(mod_x[:, None, :], 6, axis=-1)
        csh1, csc1, cg1, csh2, csc2, cg2 = jnp.split(mod_c, 6, axis=-1)

        hx = rmsnorm(x, norm1_g[i]) * (1.0 + sc1) + sh1
        hc = rmsnorm(ctx, norm1_g[i]) * (1.0 + csc1) + csh1
        px = hx @ w_in[i]
        pc = hc @ (w_in[i][:, :hy_lo] if last else w_in[i])

        kn_c, kr_c, v_c = mla_kv(pc[..., MLA_Q_RANK:N_MLA_IN], mla_kv_norm_g[i], mla_w_ukv[i])
        kn_x, kr_x, v_x = mla_kv(px[..., MLA_Q_RANK:N_MLA_IN], mla_kv_norm_g[i], mla_w_ukv[i])
        qn_x, qr_x = mla_q(px[..., :MLA_Q_RANK], mla_q_norm_g[i], mla_w_uq[i])
        att_x = block_attention(qn_x, axial_rope(qr_x, ang),
                                jnp.concatenate([kn_c, kn_x], axis=1),
                                jnp.concatenate([kr_c, axial_rope(kr_x, ang)], axis=1),
                                jnp.concatenate([v_c, v_x], axis=1))

        ml_x, ml_c = mlstm_mixer(px[..., ml_lo:hy_lo], pc[..., ml_lo:hy_lo], ml_conv_w[i], ml_conv_b[i],
                                 ml_wq[i], ml_wk[i], ml_gate_b[i], ml_norm_g[i], not last)

        hy_x = hyena_mixer(px[..., hy_lo:], hy_conv_w[i], hy_conv_b[i], hy_w1[i], hy_b1[i], hy_w2[i],
                           hy_b2[i], hy_w3[i], hy_sin_freq[i], hy_bias_d[i])

        x = x + g1 * (jnp.concatenate([att_x, ml_x, hy_x], axis=-1) @ w_out[i])
        x = x + g2 * conv_ffn(rmsnorm(x, norm2_g[i]) * (1.0 + sc2) + sh2,
                              ffn_w_up[i], ffn_conv_w[i], ffn_conv_b[i], ffn_w_down[i])

        if not last:
            qn_c, qr_c = mla_q(pc[..., :MLA_Q_RANK], mla_q_norm_g[i], mla_w_uq[i])
            att_c = block_attention(qn_c, qr_c, kn_c, kr_c, v_c)
            hy_c = hyena_mixer(pc[..., hy_lo:], hy_conv_w[i], hy_conv_b[i], hy_w1[i], hy_b1[i], hy_w2[i],
                               hy_b2[i], hy_w3[i], hy_sin_freq[i], hy_bias_d[i])
            ctx = ctx + cg1 * (jnp.concatenate([att_c, ml_c, hy_c], axis=-1) @ w_out[i])
            ctx = ctx + cg2 * conv_ffn(rmsnorm(ctx, norm2_g[i]) * (1.0 + csc2) + csh2,
                                       ffn_w_up[i], ffn_conv_w[i], ffn_conv_b[i], ffn_w_down[i])
    return rmsnorm(x, final_norm_g)
```

```python
import functools
import math

import numpy as np
import jax
import jax.numpy as jnp
from jax import lax
from jax.experimental import pallas as pl
from jax.experimental.pallas import tpu as pltpu

F32 = jnp.float32
BF16 = jnp.bfloat16
HIGHEST = lax.Precision.HIGHEST

D_MODEL = 1024
DEPTH = 2
GRID_W = 64
EPS = 1e-6
MLA_HEADS = 8
MLA_NOPE = 64
MLA_ROPE = 32
MLA_V = 64
MLA_Q_RANK = 384
MLA_KV_RANK = 256
ROPE_THETA = 10000.0
ML_HEADS = 4
ML_HEAD_DIM = 64
ML_WIDTH = 256
HY_WIDTH = 256
HY_BANDS = 16
HY_DECAY_TARGET = 1e-2
HY_FAST_PCT = 0.3
HY_SLOW_PCT = 1.5
HY_SHIFT = 0.05
N_MLA_IN = MLA_Q_RANK + MLA_KV_RANK + MLA_ROPE
N_ML_IN = 3 * ML_WIDTH + 4 * ML_HEADS
D_FF = 2816

LANES = 128
HEAD_PAD = 128
SEG = 768
ROPE_LANE = 64
GATE_LANE = 32
ML_CHUNK = 128
FF_CHUNK = 256
VMEM_LIMIT = 56 * 1024 * 1024


def _cparams(sem):
    return pltpu.CompilerParams(dimension_semantics=sem, vmem_limit_bytes=VMEM_LIMIT)


def _lane_iota(shape):
    return lax.broadcasted_iota(jnp.int32, shape, len(shape) - 1)


def _in_proj_kernel(x_ref, g_ref, sc_ref, sh_ref, w_ref, o_mla, o_ml, o_hy):
    x = x_ref[0]
    ms = jnp.mean(x * x, axis=-1, keepdims=True)
    h = x * lax.rsqrt(ms + EPS) * g_ref[...]
    hb = (h * (1.0 + sc_ref[0]) + sh_ref[0]).astype(BF16)
    for i, o_ref in enumerate((o_mla, o_ml, o_hy)):
        y = jnp.dot(hb, w_ref[:, i * SEG:(i + 1) * SEG], preferred_element_type=F32)
        o_ref[0] = y.astype(o_ref.dtype)


def _in_proj(x, g, sc, sh, w, tm):
    b, seq, d = x.shape
    bm = sc.shape[0]
    mod_map = (lambda i, t: (i, 0, 0)) if bm > 1 else (lambda i, t: (0, 0, 0))
    out_spec = pl.BlockSpec((1, tm, SEG), lambda i, t: (i, t, 0))
    return pl.pallas_call(
        _in_proj_kernel,
        grid=(b, seq // tm),
        in_specs=[
            pl.BlockSpec((1, tm, d), lambda i, t: (i, t, 0)),
            pl.BlockSpec((1, d), lambda i, t: (0, 0)),
            pl.BlockSpec((1, 1, d), mod_map),
            pl.BlockSpec((1, 1, d), mod_map),
            pl.BlockSpec((d, 3 * SEG), lambda i, t: (0, 0)),
        ],
        out_specs=[out_spec, out_spec, out_spec],
        out_shape=[
            jax.ShapeDtypeStruct((b, seq, SEG), F32),
            jax.ShapeDtypeStruct((b, seq, SEG), BF16),
            jax.ShapeDtypeStruct((b, seq, SEG), BF16),
        ],
        compiler_params=_cparams(("parallel", "parallel")),
        name="in_proj",
    )(x, g, sc, sh, w)


def _swap_rope_halves(x):
    width = x.shape[-1]
    lane = _lane_iota(x.shape) % HEAD_PAD
    lo = pltpu.roll(x, width - MLA_ROPE // 2, x.ndim - 1)
    hi = pltpu.roll(x, MLA_ROPE // 2, x.ndim - 1)
    return jnp.where(lane < ROPE_LANE + MLA_ROPE // 2, lo, hi)


def _mla_proj_kernel(a_ref, gq_ref, gkv_ref, wq_ref, wk_ref, wv_ref, cos_ref, sin_ref,
                     q_ref, k_ref, v_ref):
    a = a_ref[0]
    cos = cos_ref[...]
    sin = sin_ref[...]
    scale = (MLA_NOPE + MLA_ROPE) ** -0.5

    ql = a[:, :MLA_Q_RANK]
    qn = ql * lax.rsqrt(jnp.mean(ql * ql, axis=-1, keepdims=True) + EPS) * gq_ref[...]
    q = jnp.dot(qn.astype(BF16), wq_ref[...], preferred_element_type=F32)
    cos8 = jnp.concatenate([cos] * MLA_HEADS, axis=-1)
    sin8 = jnp.concatenate([sin] * MLA_HEADS, axis=-1)
    q = (q * cos8 + _swap_rope_halves(q) * sin8) * scale
    q_ref[0] = q.astype(q_ref.dtype)

    kvl = a[:, MLA_Q_RANK:MLA_Q_RANK + MLA_KV_RANK]
    kvn = kvl * lax.rsqrt(jnp.mean(kvl * kvl, axis=-1, keepdims=True) + EPS) * gkv_ref[...]
    kvb = kvn.astype(BF16)
    kn = jnp.dot(kvb, wk_ref[...], preferred_element_type=F32)
    v_ref[0] = jnp.dot(kvb, wv_ref[...], preferred_element_type=F32).astype(v_ref.dtype)

    blk = a[:, MLA_Q_RANK + MLA_KV_RANK:]
    kr = jnp.where(_lane_iota(blk.shape) < MLA_ROPE, blk, 0.0)
    kr = pltpu.roll(kr, ROPE_LANE, 1)
    kr = kr * cos + _swap_rope_halves(kr) * sin
    k = kn + jnp.concatenate([kr] * MLA_HEADS, axis=-1)
    k_ref[0] = k.astype(k_ref.dtype)


def _mla_proj(a_mla, gq, gkv, wq, wk, wv, cos, sin, tm):
    b, seq, _ = a_mla.shape
    hw = MLA_HEADS * HEAD_PAD
    vw = MLA_HEADS * MLA_V
    const = lambda i, t: (0, 0)
    return pl.pallas_call(
        _mla_proj_kernel,
        grid=(b, seq // tm),
        in_specs=[
            pl.BlockSpec((1, tm, SEG), lambda i, t: (i, t, 0)),
            pl.BlockSpec((1, MLA_Q_RANK), const),
            pl.BlockSpec((1, MLA_KV_RANK), const),
            pl.BlockSpec((MLA_Q_RANK, hw), const),
            pl.BlockSpec((MLA_KV_RANK, hw), const),
            pl.BlockSpec((MLA_KV_RANK, vw), const),
            pl.BlockSpec((tm, HEAD_PAD), lambda i, t: (t, 0)),
            pl.BlockSpec((tm, HEAD_PAD), lambda i, t: (t, 0)),
        ],
        out_specs=[
            pl.BlockSpec((1, tm, hw), lambda i, t: (i, t, 0)),
            pl.BlockSpec((1, tm, hw), lambda i, t: (i, t, 0)),
            pl.BlockSpec((1, tm, vw), lambda i, t: (i, t, 0)),
        ],
        out_shape=[
            jax.ShapeDtypeStruct((b, seq, hw), BF16),
            jax.ShapeDtypeStruct((b, seq, hw), BF16),
            jax.ShapeDtypeStruct((b, seq, vw), BF16),
        ],
        compiler_params=_cparams(("parallel", "parallel")),
        name="mla_proj",
    )(a_mla, gq, gkv, wq, wk, wv, cos, sin)


def _attn_kernel(*refs, tk, n_kx):
    if n_kx:
        q_ref, kc_ref, vc_ref, kx_ref, vx_ref, o_ref = refs
    else:
        q_ref, kc_ref, vc_ref, o_ref = refs
    tq = q_ref.shape[1]
    nt_dims = (((1,), (1,)), ((), ()))
    outs = []
    for hh in range(2):
        q = q_ref[0, :, hh * HEAD_PAD:(hh + 1) * HEAD_PAD]

        def step(k, v, carry):
            m, l, acc = carry
            s = lax.dot_general(q, k, nt_dims, preferred_element_type=F32)
            m_new = jnp.maximum(m, jnp.max(s, axis=-1, keepdims=True))
            p = jnp.exp(s - m_new)
            alpha = jnp.exp(m - m_new)
            l = alpha * l + jnp.sum(p, axis=-1, keepdims=True)
            acc = alpha * acc + jnp.dot(p.astype(BF16), v, preferred_element_type=F32)
            return m_new, l, acc

        carry = (jnp.full((tq, 1), -jnp.inf, F32), jnp.zeros((tq, 1), F32),
                 jnp.zeros((tq, 2 * MLA_V), F32))
        carry = step(kc_ref[0, :, hh * HEAD_PAD:(hh + 1) * HEAD_PAD], vc_ref[0], carry)
        if n_kx:
            def body(i, c):
                rows = pl.ds(pl.multiple_of(i * tk, tk), tk)
                return step(kx_ref[0, rows, hh * HEAD_PAD:(hh + 1) * HEAD_PAD],
                            vx_ref[0, rows, :], c)
            carry = lax.fori_loop(0, n_kx, body, carry)
        _, l, acc = carry
        outs.append(acc / l)
    lane = _lane_iota(outs[0].shape)
    o_ref[0] = jnp.where(lane < MLA_V, outs[0], outs[1]).astype(o_ref.dtype)


def _attention(q, kc, vc, kx, vx, tq, tk):
    b, lq, _ = q.shape
    lc = kc.shape[1]
    n_pairs = MLA_HEADS // 2
    in_specs = [
        pl.BlockSpec((1, tq, 2 * HEAD_PAD), lambda i, p, t: (i, t, p)),
        pl.BlockSpec((1, lc, 2 * HEAD_PAD), lambda i, p, t: (i, 0, p)),
        pl.BlockSpec((1, lc, 2 * MLA_V), lambda i, p, t: (i, 0, p)),
    ]
    args = [q, kc, vc]
    n_kx = 0
    if kx is not None:
        lx = kx.shape[1]
        n_kx = lx // tk
        in_specs += [
            pl.BlockSpec((1, lx, 2 * HEAD_PAD), lambda i, p, t: (i, 0, p)),
            pl.BlockSpec((1, lx, 2 * MLA_V), lambda i, p, t: (i, 0, p)),
        ]
        args += [kx, vx]
    return pl.pallas_call(
        functools.partial(_attn_kernel, tk=tk, n_kx=n_kx),
        grid=(b, n_pairs, lq // tq),
        in_specs=in_specs,
        out_specs=pl.BlockSpec((1, tq, 2 * MLA_V), lambda i, p, t: (i, t, p)),
        out_shape=jax.ShapeDtypeStruct((b, lq, MLA_HEADS * MLA_V), BF16),
        compiler_params=_cparams(("parallel", "parallel", "arbitrary")),
        name="attention",
    )(*args)


def _halo_specs(seq, tm, halo, width):
    r = tm // halo
    last = seq // halo - 1
    prev = pl.BlockSpec((1, halo, width), lambda i, t: (i, jnp.maximum(t * r - 1, 0), 0))
    nxt = pl.BlockSpec((1, halo, width), lambda i, t: (i, jnp.minimum((t + 1) * r, last), 0))
    return prev, nxt


def _conv3(ext_ref, halo, tm, w, b):
    return (b + w[0:1] * ext_ref[pl.ds(halo - 1, tm), :]
            + w[1:2] * ext_ref[pl.ds(halo, tm), :]
            + w[2:3] * ext_ref[pl.ds(halo + 1, tm), :])


def _fill_ext(ext_ref, prev, cur, nxt, halo, tm):
    t = pl.program_id(1)
    first = t == 0
    last = t == pl.num_programs(1) - 1
    ext_ref[0:halo, :] = jnp.where(first, 0.0, prev)
    ext_ref[halo:halo + tm, :] = cur
    ext_ref[halo + tm:, :] = jnp.where(last, 0.0, nxt)


def _ml_feat_kernel(pp_ref, p_ref, pn_ref, cw_ref, cb_ref, wq_ref, wk_ref, q_ref, k_ref,
                    ext_ref, *, halo):
    tm = p_ref.shape[1]
    _fill_ext(ext_ref, pp_ref[0].astype(F32), p_ref[0].astype(F32), pn_ref[0].astype(F32),
              halo, tm)
    u = jax.nn.silu(_conv3(ext_ref, halo, tm, cw_ref[...], cb_ref[...])).astype(BF16)
    q_ref[0] = jnp.dot(u, wq_ref[...], preferred_element_type=F32).astype(q_ref.dtype)
    k_ref[0] = jnp.dot(u, wk_ref[...], preferred_element_type=F32).astype(k_ref.dtype)


def _ml_features(a_ml, cw, cb, wq_bd, wk_bd, tm):
    b, seq, _ = a_ml.shape
    halo = 16
    prev, nxt = _halo_specs(seq, tm, halo, ML_WIDTH)
    const = lambda i, t: (0, 0)
    out_spec = pl.BlockSpec((1, tm, ML_WIDTH), lambda i, t: (i, t, 0))
    return pl.pallas_call(
        functools.partial(_ml_feat_kernel, halo=halo),
        grid=(b, seq // tm),
        in_specs=[
            prev,
            pl.BlockSpec((1, tm, ML_WIDTH), lambda i, t: (i, t, 0)),
            nxt,
            pl.BlockSpec((3, ML_WIDTH), const),
            pl.BlockSpec((1, ML_WIDTH), const),
            pl.BlockSpec((ML_WIDTH, ML_WIDTH), const),
            pl.BlockSpec((ML_WIDTH, ML_WIDTH), const),
        ],
        out_specs=[out_spec, out_spec],
        out_shape=[jax.ShapeDtypeStruct((b, seq, ML_WIDTH), BF16)] * 2,
        scratch_shapes=[pltpu.VMEM((tm + 2 * halo, ML_WIDTH), F32)],
        compiler_params=_cparams(("parallel", "arbitrary")),
        name="ml_features",
    )(a_ml, a_ml, a_ml, cw, cb, wq_bd, wk_bd)


def _log_sigmoid(x):
    return jnp.minimum(x, 0.0) - jnp.log1p(jnp.exp(-jnp.abs(x)))


def _ml_chunk(q, k, v, gates, gate_b, state_ref, m_ref, direction):
    t_len = q.shape[0]
    x = gates + gate_b
    row = lax.broadcasted_iota(jnp.int32, (t_len, t_len), 0)
    col = lax.broadcasted_iota(jnp.int32, (t_len, t_len), 1)
    keep = (col <= row) if direction == 0 else (col >= row)
    tri = keep.astype(F32)
    cum = jnp.dot(tri, _log_sigmoid(x), precision=HIGHEST, preferred_element_type=F32)
    lane = _lane_iota(x.shape)
    f_lo = GATE_LANE + (2 * direction + 1) * ML_HEADS
    packed = jnp.where((lane >= f_lo) & (lane < f_lo + ML_HEADS), cum, x)
    packed_t = packed.T
    lane_p = _lane_iota((t_len, 2 * ML_HEAD_DIM))
    last = t_len - 1 if direction == 0 else 0
    outs = []
    for h in range(ML_HEADS):
        pair, odd = divmod(h, 2)
        sl = slice(pair * 2 * ML_HEAD_DIM, (pair + 1) * 2 * ML_HEAD_DIM)
        mine = (lane_p >= ML_HEAD_DIM) if odd else (lane_p < ML_HEAD_DIM)
        one_lane = 0 if odd else ML_HEAD_DIM
        li_l = GATE_LANE + 2 * direction * ML_HEADS + h
        cu_l = f_lo + h
        li_c = packed[:, li_l:li_l + 1]
        cu_c = packed[:, cu_l:cu_l + 1]
        li_r = packed_t[li_l:li_l + 1, :]
        cu_r = packed_t[cu_l:cu_l + 1, :]
        m = m_ref[h]
        state = state_ref[h]

        dlog = jnp.where(keep, cu_c - cu_r + li_r, -jnp.inf)
        inter = cu_c + m
        m_out = jnp.maximum(inter, jnp.max(dlog, axis=-1, keepdims=True))
        qh = jnp.where(mine, q[:, sl], 0).astype(BF16)
        kh = k[:, sl]
        vh = jnp.where(mine, v[:, sl].astype(F32), (lane_p == one_lane).astype(F32))
        vh = vh.astype(BF16)
        s = lax.dot_general(qh, kh, (((1,), (1,)), ((), ())), preferred_element_type=F32)
        s = s * jnp.exp(dlog - m_out)
        a = jnp.exp(inter - m_out)
        sv = jnp.dot(s.astype(BF16), vh, preferred_element_type=F32)
        qc = jnp.dot(qh, state.astype(BF16), preferred_element_type=F32)
        den = jnp.sum(s, axis=-1, keepdims=True) + a * qc[:, one_lane:one_lane + 1]
        outs.append((sv + a * qc) / jnp.maximum(jnp.abs(den), jnp.exp(-m_out)))

        cum_last = cu_c[last:last + 1, :]
        g = cum_last - cu_c + li_c
        m_new = jnp.maximum(cum_last + m, jnp.max(g, axis=0, keepdims=True))
        wts = jnp.exp(g - m_new)
        decay = jnp.exp(cum_last + m - m_new)
        kw_t = (jnp.where(mine, kh.astype(F32), 0.0) * wts).T.astype(BF16)
        upd = jnp.dot(kw_t, vh, preferred_element_type=F32)
        state_ref[h] = decay * state + upd
        m_ref[h] = m_new
    pairs = []
    for pair in range(ML_HEADS // 2):
        pairs.append(jnp.where(lane_p < ML_HEAD_DIM, outs[2 * pair], outs[2 * pair + 1]))
    return jnp.concatenate(pairs, axis=-1)


def _ml_scan_kernel(qc_ref, kc_ref, vc_ref, oc_ref, gc_ref, qx_ref, kx_ref, vx_ref, ox_ref,
                    gx_ref, gb_ref, ng_ref, bd_ref, hx_ref, hc_ref, fx_scr, fc_scr,
                    state_scr, m_scr):
    t_len = ML_CHUNK
    n_c = qc_ref.shape[1] // t_len
    n_x = qx_ref.shape[1] // t_len
    gate_b = gb_ref[...]

    def reset():
        state_scr[...] = jnp.zeros_like(state_scr)
        m_scr[...] = jnp.zeros_like(m_scr)

    def chunk(refs, i, direction):
        q_ref, k_ref, v_ref, _, g_ref = refs
        rows = pl.ds(pl.multiple_of(i * t_len, t_len), t_len)
        return _ml_chunk(q_ref[0, rows, :], k_ref[0, rows, :], v_ref[0, rows, :],
                         g_ref[0, rows, :], gate_b, state_scr, m_scr, direction)

    def finish(h, o_gate):
        y = h * jax.nn.sigmoid(o_gate.astype(F32))
        ms = jnp.dot(y * y, bd_ref[...], precision=HIGHEST, preferred_element_type=F32)
        return y * lax.rsqrt(ms + EPS) * ng_ref[...]

    ctx = (qc_ref, kc_ref, vc_ref, oc_ref, gc_ref)
    lat = (qx_ref, kx_ref, vx_ref, ox_ref, gx_ref)

    reset()

    def fwd_c(i, _):
        fc_scr[pl.ds(pl.multiple_of(i * t_len, t_len), t_len), :] = chunk(ctx, i, 0)
        return 0

    def fwd_x(i, _):
        fx_scr[pl.ds(pl.multiple_of(i * t_len, t_len), t_len), :] = chunk(lat, i, 0)
        return 0

    lax.fori_loop(0, n_c, fwd_c, 0)
    lax.fori_loop(0, n_x, fwd_x, 0)

    reset()

    def bwd_c(j, _):
        i = n_c - 1 - j
        rows = pl.ds(pl.multiple_of(i * t_len, t_len), t_len)
        h = fc_scr[rows, :] + chunk(ctx, i, 1)
        hc_ref[0, rows, :] = finish(h, oc_ref[0, rows, :]).astype(hc_ref.dtype)
        return 0

    def bwd_x(j, _):
        i = n_x - 1 - j
        rows = pl.ds(pl.multiple_of(i * t_len, t_len), t_len)
        h = fx_scr[rows, :] + chunk(lat, i, 1)
        hx_ref[0, rows, :] = finish(h, ox_ref[0, rows, :]).astype(hx_ref.dtype)
        return 0

    lax.fori_loop(0, n_c, bwd_c, 0)
    lax.fori_loop(0, n_x, bwd_x, 0)


def _ml_scan(qc, kc, a_ml_c, a_mla_c, qx, kx, a_ml_x, a_mla_x, gate_b, norm_g, bd):
    b, lc, _ = qc.shape
    lx = qx.shape[1]
    w = ML_WIDTH
    gate_block = (SEG - LANES) // LANES

    def seq_specs(n):
        return [
            pl.BlockSpec((1, n, w), lambda i: (i, 0, 0)),
            pl.BlockSpec((1, n, w), lambda i: (i, 0, 0)),
            pl.BlockSpec((1, n, w), lambda i: (i, 0, 1)),
            pl.BlockSpec((1, n, w), lambda i: (i, 0, 2)),
            pl.BlockSpec((1, n, LANES), lambda i: (i, 0, gate_block)),
        ]

    const = lambda i: (0, 0)
    return pl.pallas_call(
        _ml_scan_kernel,
        grid=(b,),
        in_specs=seq_specs(lc) + seq_specs(lx) + [
            pl.BlockSpec((1, LANES), const),
            pl.BlockSpec((1, w), const),
            pl.BlockSpec((w, w), const),
        ],
        out_specs=[
            pl.BlockSpec((1, lx, w), lambda i: (i, 0, 0)),
            pl.BlockSpec((1, lc, w), lambda i: (i, 0, 0)),
        ],
        out_shape=[
            jax.ShapeDtypeStruct((b, lx, w), BF16),
            jax.ShapeDtypeStruct((b, lc, w), BF16),
        ],
        scratch_shapes=[
            pltpu.VMEM((lx, w), F32),
            pltpu.VMEM((lc, w), F32),
            pltpu.VMEM((ML_HEADS, 2 * ML_HEAD_DIM, 2 * ML_HEAD_DIM), F32),
            pltpu.VMEM((ML_HEADS, 1, 1), F32),
        ],
        compiler_params=_cparams(("parallel",)),
        name="ml_scan",
    )(qc, kc, a_ml_c, a_ml_c, a_mla_c, qx, kx, a_ml_x, a_ml_x, a_mla_x, gate_b, norm_g, bd)


def _hy_pre_kernel(pp_ref, p_ref, pn_ref, cw_ref, cb_ref, x0_ref, z_ref, ext_ref, *, halo):
    tm = p_ref.shape[1]
    _fill_ext(ext_ref, pp_ref[0].astype(F32), p_ref[0].astype(F32), pn_ref[0].astype(F32),
              halo, tm)
    u = _conv3(ext_ref, halo, tm, cw_ref[...], cb_ref[...])
    x0_ref[0] = u[:, :HY_WIDTH].astype(x0_ref.dtype)
    z_ref[...] = (u[:, HY_WIDTH:2 * HY_WIDTH] * u[:, 2 * HY_WIDTH:]).astype(z_ref.dtype)


def _hy_pre(a_hy, cw, cb, tm):
    b, seq, width = a_hy.shape
    halo = 16
    prev, nxt = _halo_specs(seq, tm, halo, width)
    const = lambda i, t: (0, 0)
    return pl.pallas_call(
        functools.partial(_hy_pre_kernel, halo=halo),
        grid=(b, seq // tm),
        in_specs=[
            prev,
            pl.BlockSpec((1, tm, width), lambda i, t: (i, t, 0)),
            nxt,
            pl.BlockSpec((3, width), const),
            pl.BlockSpec((1, width), const),
        ],
        out_specs=[
            pl.BlockSpec((1, tm, HY_WIDTH), lambda i, t: (i, t, 0)),
            pl.BlockSpec((tm, HY_WIDTH), lambda i, t: (t, i)),
        ],
        out_shape=[
            jax.ShapeDtypeStruct((b, seq, HY_WIDTH), BF16),
            jax.ShapeDtypeStruct((seq, b * HY_WIDTH), BF16),
        ],
        scratch_shapes=[pltpu.VMEM((tm + 2 * halo, width), F32)],
        compiler_params=_cparams(("parallel", "arbitrary")),
        name="hy_pre",
    )(a_hy, a_hy, a_hy, cw, cb)


@functools.lru_cache(maxsize=None)
def _dft_mats(seq, kt):
    n_fft = 2 * seq
    k = np.arange(seq, dtype=np.int64)[:, None]
    n = np.arange(seq, dtype=np.int64)[None, :]
    ang = (2.0 * np.pi / n_fft) * ((k * n) % n_fft).astype(np.float64)
    cos = np.cos(ang)
    msin = -np.sin(ang)
    msin[0, :] = 1.0 - 2.0 * (np.arange(seq) % 2)
    fwd = np.stack([cos.reshape(seq // kt, kt, seq), msin.reshape(seq // kt, kt, seq)], axis=1)
    fwd = fwd.reshape(2 * seq, seq)
    wgt = np.full((seq, 1), 2.0 / n_fft)
    wgt[0, 0] = 1.0 / n_fft
    inv = np.stack([(cos * wgt).reshape(seq // kt, kt, seq),
                    (msin * wgt).reshape(seq // kt, kt, seq)], axis=1)
    inv = inv.reshape(2 * seq, seq).T
    return jnp.asarray(fwd, dtype=BF16), jnp.asarray(np.ascontiguousarray(inv), dtype=BF16)


def _dft_fwd_kernel(f_ref, z_ref, *rest, kt, reps):
    zz = jnp.dot(f_ref[...], z_ref[...], preferred_element_type=F32)
    if not rest[1:]:
        rest[0][...] = zz
        return
    hre_ref, him_ref, hre2_ref, y_ref = rest
    tile = lambda r: jnp.concatenate([r[...]] * reps, axis=-1)
    hre, him, hre2 = tile(hre_ref), tile(him_ref), tile(hre2_ref)
    zre, zim = zz[:kt], zz[kt:]
    y_ref[:kt, :] = (zre * hre - zim * him).astype(y_ref.dtype)
    y_ref[kt:, :] = (zre * him + zim * hre2).astype(y_ref.dtype)


def _dft_fwd(fwd, z, spectrum, kt, cb):
    seq = z.shape[0]
    ncol = z.shape[1]
    in_specs = [
        pl.BlockSpec((2 * kt, seq), lambda i, j: (i, 0)),
        pl.BlockSpec((seq, cb), lambda i, j: (0, j)),
    ]
    args = [fwd, z]
    out_dtype = F32
    if spectrum is not None:
        in_specs += [pl.BlockSpec((kt, HY_WIDTH), lambda i, j: (i, 0))] * 3
        args += list(spectrum)
        out_dtype = BF16
    return pl.pallas_call(
        functools.partial(_dft_fwd_kernel, kt=kt, reps=cb // HY_WIDTH),
        grid=(seq // kt, ncol // cb),
        in_specs=in_specs,
        out_specs=pl.BlockSpec((2 * kt, cb), lambda i, j: (i, j)),
        out_shape=jax.ShapeDtypeStruct((2 * seq, ncol), out_dtype),
        compiler_params=_cparams(("parallel", "arbitrary")),
        name="dft_fwd",
    )(*args)


def _dft_inv_kernel(g_ref, y_ref, z_ref, x0_ref, bias_ref, o_ref, *, nb):
    y = jnp.dot(g_ref[...], y_ref[...], preferred_element_type=F32)
    bias = bias_ref[...]
    for bb in range(nb):
        cols = slice(bb * HY_WIDTH, (bb + 1) * HY_WIDTH)
        yb = y[:, cols] + z_ref[:, cols].astype(F32) * bias
        o_ref[bb] = (x0_ref[bb].astype(F32) * yb).astype(o_ref.dtype)


def _dft_inv(inv, y, z, x0, bias, tm, nb):
    b, seq, _ = x0.shape
    cb = nb * HY_WIDTH
    return pl.pallas_call(
        functools.partial(_dft_inv_kernel, nb=nb),
        grid=(seq // tm, b // nb),
        in_specs=[
            pl.BlockSpec((tm, 2 * seq), lambda t, j: (t, 0)),
            pl.BlockSpec((2 * seq, cb), lambda t, j: (0, j)),
            pl.BlockSpec((tm, cb), lambda t, j: (t, j)),
            pl.BlockSpec((nb, tm, HY_WIDTH), lambda t, j: (j, t, 0)),
            pl.BlockSpec((1, HY_WIDTH), lambda t, j: (0, 0)),
        ],
        out_specs=pl.BlockSpec((nb, tm, HY_WIDTH), lambda t, j: (j, t, 0)),
        out_shape=jax.ShapeDtypeStruct((b, seq, HY_WIDTH), BF16),
        compiler_params=_cparams(("parallel", "arbitrary")),
        name="dft_inv",
    )(inv, y, z, x0, bias)


def _hyena_filter_taps(seq, w1, b1, w2, b2, w3, sin_freq):
    t = jnp.linspace(0.0, 1.0, seq, dtype=F32)[:, None]
    omega = 2.0 * math.pi * jnp.arange(seq, dtype=F32) / seq
    bands = jnp.linspace(1e-4, HY_BANDS - 1, HY_BANDS, dtype=F32)
    ang = omega[:, None] * bands[None, :]
    z = jnp.concatenate([t, jnp.cos(ang), -jnp.sin(ang)], axis=-1)
    hdn = jnp.sin(sin_freq * (jnp.dot(z, w1, precision=HIGHEST) + b1))
    hdn = jnp.sin(sin_freq * (jnp.dot(hdn, w2, precision=HIGHEST) + b2))
    filt = jnp.dot(hdn, w3, precision=HIGHEST)
    deltas = jnp.abs(jnp.linspace(math.log(HY_DECAY_TARGET) / HY_SLOW_PCT,
                                  math.log(HY_DECAY_TARGET) / HY_FAST_PCT, HY_WIDTH, dtype=F32))
    window = jnp.exp(-t * deltas) + HY_SHIFT
    h_fwd = filt[:, :HY_WIDTH] * window
    h_bwd = (filt[:, HY_WIDTH:] * window).at[0].set(0.0)
    l1 = jnp.sum(jnp.abs(h_fwd), axis=0) + jnp.sum(jnp.abs(h_bwd), axis=0)
    return h_fwd, h_bwd, l1


def _hyena_spectrum(seq, kt, fwd, filt_w):
    h_fwd, h_bwd, l1 = _hyena_filter_taps(seq, *filt_w)
    taps = jnp.concatenate([h_fwd, h_bwd], axis=-1).astype(BF16)
    raw = _dft_fwd(fwd, taps, None, kt, 2 * HY_WIDTH)
    raw = raw.reshape(seq // kt, 2, kt, 2 * HY_WIDTH)
    hc = raw[:, 0].reshape(seq, 2 * HY_WIDTH)
    hs = raw[:, 1].reshape(seq, 2 * HY_WIDTH)
    inv_l1 = 1.0 / l1
    hre = (hc[:, :HY_WIDTH] + hc[:, HY_WIDTH:]) * inv_l1
    him = ((hs[:, :HY_WIDTH] - hs[:, HY_WIDTH:]) * inv_l1).at[0].set(0.0)
    nyq = (hs[0, :HY_WIDTH] + hs[0, HY_WIDTH:]) * inv_l1
    hre2 = hre.at[0].set(nyq)
    return hre, him, hre2


def _hyena(a_hy, cw, cb, filt_w, bias_d, tm, kt, nb):
    seq = a_hy.shape[1]
    fwd, inv = _dft_mats(seq, kt)
    x0, z = _hy_pre(a_hy, cw, cb, tm)
    spectrum = _hyena_spectrum(seq, kt, fwd, filt_w)
    y = _dft_fwd(fwd, z, spectrum, kt, nb * HY_WIDTH)
    return _dft_inv(inv, y, z, x0, bias_d, tm, nb)


def _out_proj_kernel(x_ref, att_ref, ml_ref, hy_ref, w_ref, g_ref, o_ref):
    wa = MLA_HEADS * MLA_V
    acc = jnp.dot(att_ref[0], w_ref[:wa, :], preferred_element_type=F32)
    acc += jnp.dot(ml_ref[0], w_ref[wa:wa + ML_WIDTH, :], preferred_element_type=F32)
    acc += jnp.dot(hy_ref[0], w_ref[wa + ML_WIDTH:, :], preferred_element_type=F32)
    o_ref[0] = x_ref[0] + g_ref[0] * acc


def _out_proj(x, att, ml, hy, w, gate, tm):
    b, seq, d = x.shape
    bm = gate.shape[0]
    mod_map = (lambda i, t: (i, 0, 0)) if bm > 1 else (lambda i, t: (0, 0, 0))
    tok = lambda width: pl.BlockSpec((1, tm, width), lambda i, t: (i, t, 0))
    return pl.pallas_call(
        _out_proj_kernel,
        grid=(b, seq // tm),
        in_specs=[tok(d), tok(MLA_HEADS * MLA_V), tok(ML_WIDTH), tok(HY_WIDTH),
                  pl.BlockSpec((d, d), lambda i, t: (0, 0)),
                  pl.BlockSpec((1, 1, d), mod_map)],
        out_specs=tok(d),
        out_shape=jax.ShapeDtypeStruct((b, seq, d), F32),
        compiler_params=_cparams(("parallel", "parallel")),
        name="out_proj",
    )(x, att, ml, hy, w, gate)


def _ffn_kernel(xp_ref, x_ref, xn_ref, g_ref, sc_ref, sh_ref, gate_ref, wup_ref, cw_ref,
                cb_ref, wdn_ref, fg_ref, o_ref, xe_scr, u_scr, acc_scr, *, halo, final_norm):
    tm = x_ref.shape[1]
    n_chunks = wup_ref.shape[0]
    t = pl.program_id(1)

    def norm_mod(x):
        ms = jnp.mean(x * x, axis=-1, keepdims=True)
        return x * lax.rsqrt(ms + EPS) * g_ref[...] * (1.0 + sc_ref[0]) + sh_ref[0]

    prev = jnp.where(t == 0, 0.0, norm_mod(xp_ref[0]))
    nxt = jnp.where(t == pl.num_programs(1) - 1, 0.0, norm_mod(xn_ref[0]))
    xe = jnp.concatenate([prev, norm_mod(x_ref[0]), nxt], axis=0)
    xe_scr[...] = xe.astype(BF16)
    acc_scr[...] = jnp.zeros_like(acc_scr)

    def body(c, _):
        u_scr[...] = jnp.dot(xe_scr[...], wup_ref[c], preferred_element_type=F32)
        uc = _conv3(u_scr, halo, tm, cw_ref[c], cb_ref[c])
        act = jax.nn.silu(uc[:, :FF_CHUNK]) * uc[:, FF_CHUNK:]
        acc_scr[...] += jnp.dot(act.astype(BF16), wdn_ref[c], preferred_element_type=F32)
        return 0

    lax.fori_loop(0, n_chunks, body, 0)
    y = x_ref[0] + gate_ref[0] * acc_scr[...]
    if final_norm:
        y = y * lax.rsqrt(jnp.mean(y * y, axis=-1, keepdims=True) + EPS) * fg_ref[...]
    o_ref[0] = y


def _conv_ffn(x, g, sc, sh, gate, wup, cw, cb, wdn, final_g, tm, final_norm):
    b, seq, d = x.shape
    halo = 8
    bm = sc.shape[0]
    mod_map = (lambda i, t: (i, 0, 0)) if bm > 1 else (lambda i, t: (0, 0, 0))
    prev, nxt = _halo_specs(seq, tm, halo, d)
    const2 = lambda i, t: (0, 0)
    const3 = lambda i, t: (0, 0, 0)
    n_chunks = wup.shape[0]
    return pl.pallas_call(
        functools.partial(_ffn_kernel, halo=halo, final_norm=final_norm),
        grid=(b, seq // tm),
        in_specs=[
            prev,
            pl.BlockSpec((1, tm, d), lambda i, t: (i, t, 0)),
            nxt,
            pl.BlockSpec((1, d), const2),
            pl.BlockSpec((1, 1, d), mod_map),
            pl.BlockSpec((1, 1, d), mod_map),
            pl.BlockSpec((1, 1, d), mod_map),
            pl.BlockSpec((n_chunks, d, 2 * FF_CHUNK), const3),
            pl.BlockSpec((n_chunks, 3, 2 * FF_CHUNK), const3),
            pl.BlockSpec((n_chunks, 1, 2 * FF_CHUNK), const3),
            pl.BlockSpec((n_chunks, FF_CHUNK, d), const3),
            pl.BlockSpec((1, d), const2),
        ],
        out_specs=pl.BlockSpec((1, tm, d), lambda i, t: (i, t, 0)),
        out_shape=jax.ShapeDtypeStruct((b, seq, d), F32),
        scratch_shapes=[
            pltpu.VMEM((tm + 2 * halo, d), BF16),
            pltpu.VMEM((tm + 2 * halo, 2 * FF_CHUNK), F32),
            pltpu.VMEM((tm, d), F32),
        ],
        compiler_params=_cparams(("parallel", "arbitrary")),
        name="conv_ffn",
    )(x, x, x, g, sc, sh, gate, wup, cw, cb, wdn, final_g)


def _layout_w_in(w):
    d = w.shape[0]
    ml_lo = N_MLA_IN
    hy_lo = N_MLA_IN + N_ML_IN
    gates = w[:, ml_lo + 3 * ML_WIDTH:hy_lo]
    pad = jnp.zeros((d, LANES - MLA_ROPE - 4 * ML_HEADS), w.dtype)
    return jnp.concatenate([w[:, :N_MLA_IN], gates, pad,
                            w[:, ml_lo:ml_lo + 3 * ML_WIDTH], w[:, hy_lo:]], axis=1).astype(BF16)


def _layout_w_uq(w):
    r = w.shape[0]
    w = w.reshape(r, MLA_HEADS, MLA_NOPE + MLA_ROPE)
    pad = jnp.zeros((r, MLA_HEADS, HEAD_PAD - MLA_NOPE - MLA_ROPE), w.dtype)
    return jnp.concatenate([w, pad], axis=-1).reshape(r, MLA_HEADS * HEAD_PAD).astype(BF16)


def _layout_w_ukv(w):
    r = w.shape[0]
    w = w.reshape(r, MLA_HEADS, MLA_NOPE + MLA_V)
    pad = jnp.zeros((r, MLA_HEADS, HEAD_PAD - MLA_NOPE), w.dtype)
    wk = jnp.concatenate([w[..., :MLA_NOPE], pad], axis=-1).reshape(r, MLA_HEADS * HEAD_PAD)
    wv = w[..., MLA_NOPE:].reshape(r, MLA_HEADS * MLA_V)
    return wk.astype(BF16), wv.astype(BF16)


def _block_diag(w):
    h, d, _ = w.shape
    eye = jnp.eye(h, dtype=w.dtype)
    return jnp.einsum('hde,hg->hdge', w, eye).reshape(h * d, h * d)


def _layout_ffn(w_up, conv_w, conv_b, w_down):
    d = w_up.shape[0]
    n = D_FF // FF_CHUNK

    def pair(a):
        lead = a.shape[:-1]
        a = a.reshape(lead + (2, n, FF_CHUNK))
        a = jnp.moveaxis(a, -3, -2)
        return a.reshape(lead + (n, 2 * FF_CHUNK))

    wup = jnp.moveaxis(pair(w_up), 1, 0).astype(BF16)
    cw = jnp.moveaxis(pair(conv_w), 1, 0)
    cb = pair(conv_b)[:, None, :]
    wdn = w_down.reshape(n, FF_CHUNK, d).astype(BF16)
    return wup, cw, cb, wdn


def _rope_tables(n_rows, rope):
    n_freq = MLA_ROPE // 4
    lanes_one = jnp.ones((n_rows * GRID_W, ROPE_LANE), F32)
    lanes_zero = jnp.zeros((n_rows * GRID_W, ROPE_LANE), F32)
    tail = jnp.zeros((n_rows * GRID_W, HEAD_PAD - ROPE_LANE - MLA_ROPE), F32)
    if not rope:
        ones = jnp.ones((n_rows * GRID_W, MLA_ROPE), F32)
        return (jnp.concatenate([lanes_one, ones, tail], axis=-1),
                jnp.concatenate([lanes_zero, 0.0 * ones, tail], axis=-1))
    inv = ROPE_THETA ** (-jnp.arange(n_freq, dtype=F32) / n_freq)
    row = jnp.repeat(jnp.arange(n_rows, dtype=F32), GRID_W)
    col = jnp.tile(jnp.arange(GRID_W, dtype=F32), n_rows)
    ang = jnp.concatenate([row[:, None] * inv, col[:, None] * inv], axis=-1)
    cos, sin = jnp.cos(ang), jnp.sin(ang)
    return (jnp.concatenate([lanes_one, cos, cos, tail], axis=-1),
            jnp.concatenate([lanes_zero, -sin, sin, tail], axis=-1))


def kernel(x, c, ctx, c_ctx, ada_w, ada_b, norm1_g, norm2_g, w_in, mla_q_norm_g, mla_kv_norm_g,
           mla_w_uq, mla_w_ukv, ml_conv_w, ml_conv_b, ml_wq, ml_wk, ml_gate_b, ml_norm_g,
           hy_conv_w, hy_conv_b, hy_w1, hy_b1, hy_w2, hy_b2, hy_w3, hy_sin_freq, hy_bias_d,
           w_out, ffn_w_up, ffn_conv_w, ffn_conv_b, ffn_w_down, final_norm_g):
    b, seq, d = x.shape
    lc = ctx.shape[1]
    depth = ada_w.shape[0]
    tmx = min(512, seq)
    tmc = min(256, lc)
    cos_x, sin_x = _rope_tables(seq // GRID_W, True)
    cos_c, sin_c = _rope_tables(lc // GRID_W, False)
    head_mean = _block_diag(jnp.full((ML_HEADS, ML_HEAD_DIM, ML_HEAD_DIM), 1.0 / ML_HEAD_DIM, F32))
    row = lambda v: v.reshape(1, -1)

    for i in range(depth):
        last = i == depth - 1
        mod_x = jnp.dot(jax.nn.silu(c), ada_w[i], precision=HIGHEST) + ada_b[i]
        mod_c = jnp.dot(jax.nn.silu(c_ctx), ada_w[i], precision=HIGHEST) + ada_b[i]
        mx = [m[:, None, :] for m in jnp.split(mod_x, 6, axis=-1)]
        mc = [m[None, None, :] for m in jnp.split(mod_c, 6, axis=-1)]

        w_in_l = _layout_w_in(w_in[i])
        wq = _layout_w_uq(mla_w_uq[i])
        wk, wv = _layout_w_ukv(mla_w_ukv[i])
        wq_bd = _block_diag(ml_wq[i]).astype(BF16)
        wk_bd = (_block_diag(ml_wk[i]) * (ML_HEAD_DIM ** -0.5)).astype(BF16)
        gate_b = jnp.zeros((1, LANES), F32).at[0, GATE_LANE:GATE_LANE + 4 * ML_HEADS].set(
            ml_gate_b[i].reshape(-1))
        wup, fcw, fcb, wdn = _layout_ffn(ffn_w_up[i], ffn_conv_w[i], ffn_conv_b[i], ffn_w_down[i])
        w_out_l = w_out[i].astype(BF16)
        filt_w = (hy_w1[i], hy_b1[i], hy_w2[i], hy_b2[i], hy_w3[i], hy_sin_freq[i])
        g1 = row(norm1_g[i])

        ax_mla, ax_ml, ax_hy = _in_proj(x, g1, mx[1], mx[0], w_in_l, tmx)
        ac_mla, ac_ml, ac_hy = _in_proj(ctx, g1, mc[1], mc[0], w_in_l, tmc)

        gq, gkv = row(mla_q_norm_g[i]), row(mla_kv_norm_g[i])
        q_x, k_x, v_x = _mla_proj(ax_mla, gq, gkv, wq, wk, wv, cos_x, sin_x, tmx)
        q_c, k_c, v_c = _mla_proj(ac_mla, gq, gkv, wq, wk, wv, cos_c, sin_c, tmc)
        att_x = _attention(q_x, k_c, v_c, k_x, v_x, tmx, min(512, seq))

        mcw, mcb = ml_conv_w[i], row(ml_conv_b[i])
        mq_x, mk_x = _ml_features(ax_ml, mcw, mcb, wq_bd, wk_bd, tmx)
        mq_c, mk_c = _ml_features(ac_ml, mcw, mcb, wq_bd, wk_bd, tmc)
        ml_x, ml_c = _ml_scan(mq_c, mk_c, ac_ml, ac_mla, mq_x, mk_x, ax_ml, ax_mla,
                              gate_b, row(ml_norm_g[i]), head_mean)

        hcw, hcb, hbias = hy_conv_w[i], row(hy_conv_b[i]), row(hy_bias_d[i])
        hy_x = _hyena(ax_hy, hcw, hcb, filt_w, hbias, tmx, min(512, seq), 2)

        x = _out_proj(x, att_x, ml_x, hy_x, w_out_l, mx[2], tmx)
        x = _conv_ffn(x, row(norm2_g[i]), mx[4], mx[3], mx[5], wup, fcw, fcb, wdn,
                      row(final_norm_g), tmx, last)

        if not last:
            att_c = _attention(q_c, k_c, v_c, None, None, tmc, tmc)
            hy_c = _hyena(ac_hy, hcw, hcb, filt_w, hbias, tmc, tmc, 2)
            ctx = _out_proj(ctx, att_c, ml_c, hy_c, w_out_l, mc[2], tmc)
            ctx = _conv_ffn(ctx, row(norm2_g[i]), mc[4], mc[3], mc[5], wup, fcw, fcb, wdn,
                            row(final_norm_g), tmc, False)
    return x
```

```python
import functools
import math

import numpy as np
import jax
import jax.numpy as jnp
from jax import lax
from jax.experimental import pallas as pl
from jax.experimental.pallas import tpu as pltpu

F32 = jnp.float32
BF16 = jnp.bfloat16
HIGHEST = lax.Precision.HIGHEST

D_MODEL = 1024
DEPTH = 2
GRID_W = 64
EPS = 1e-6
MLA_HEADS = 8
MLA_NOPE = 64
MLA_ROPE = 32
MLA_V = 64
MLA_Q_RANK = 384
MLA_KV_RANK = 256
ROPE_THETA = 10000.0
ML_HEADS = 4
ML_HEAD_DIM = 64
ML_WIDTH = 256
HY_WIDTH = 256
HY_BANDS = 16
HY_DECAY_TARGET = 1e-2
HY_FAST_PCT = 0.3
HY_SLOW_PCT = 1.5
HY_SHIFT = 0.05
N_MLA_IN = MLA_Q_RANK + MLA_KV_RANK + MLA_ROPE
N_ML_IN = 3 * ML_WIDTH + 4 * ML_HEADS
D_FF = 2816

LANES = 128
HEAD_PAD = 128
SEG = 768
ROPE_LANE = 64
GATE_LANE = 32
ML_CHUNK = 128
FF_CHUNK = 256
VMEM_LIMIT = 56 * 1024 * 1024


def _cparams(sem):
    return pltpu.CompilerParams(dimension_semantics=sem, vmem_limit_bytes=VMEM_LIMIT)


def _lane_iota(shape):
    return lax.broadcasted_iota(jnp.int32, shape, len(shape) - 1)


def _in_proj_kernel(x_ref, g_ref, sc_ref, sh_ref, w_ref, o_mla, o_ml, o_hy):
    x = x_ref[0]
    ms = jnp.mean(x * x, axis=-1, keepdims=True)
    h = x * lax.rsqrt(ms + EPS) * g_ref[...]
    hb = (h * (1.0 + sc_ref[0]) + sh_ref[0]).astype(BF16)
    for i, o_ref in enumerate((o_mla, o_ml, o_hy)):
        y = jnp.dot(hb, w_ref[:, i * SEG:(i + 1) * SEG], preferred_element_type=F32)
        o_ref[0] = y.astype(o_ref.dtype)


def _in_proj(x, g, sc, sh, w, tm):
    b, seq, d = x.shape
    bm = sc.shape[0]
    mod_map = (lambda i, t: (i, 0, 0)) if bm > 1 else (lambda i, t: (0, 0, 0))
    out_spec = pl.BlockSpec((1, tm, SEG), lambda i, t: (i, t, 0))
    return pl.pallas_call(
        _in_proj_kernel,
        grid=(b, seq // tm),
        in_specs=[
            pl.BlockSpec((1, tm, d), lambda i, t: (i, t, 0)),
            pl.BlockSpec((1, d), lambda i, t: (0, 0)),
            pl.BlockSpec((1, 1, d), mod_map),
            pl.BlockSpec((1, 1, d), mod_map),
            pl.BlockSpec((d, 3 * SEG), lambda i, t: (0, 0)),
        ],
        out_specs=[out_spec, out_spec, out_spec],
        out_shape=[
            jax.ShapeDtypeStruct((b, seq, SEG), F32),
            jax.ShapeDtypeStruct((b, seq, SEG), BF16),
            jax.ShapeDtypeStruct((b, seq, SEG), BF16),
        ],
        compiler_params=_cparams(("parallel", "parallel")),
        name="in_proj",
    )(x, g, sc, sh, w)


def _swap_rope_halves(x):
    width = x.shape[-1]
    lane = _lane_iota(x.shape) % HEAD_PAD
    lo = pltpu.roll(x, width - MLA_ROPE // 2, x.ndim - 1)
    hi = pltpu.roll(x, MLA_ROPE // 2, x.ndim - 1)
    return jnp.where(lane < ROPE_LANE + MLA_ROPE // 2, lo, hi)


def _mla_proj_kernel(a_ref, gq_ref, gkv_ref, wq_ref, wk_ref, wv_ref, cos_ref, sin_ref,
                     q_ref, k_ref, v_ref):
    a = a_ref[0]
    cos = cos_ref[...]
    sin = sin_ref[...]
    scale = (MLA_NOPE + MLA_ROPE) ** -0.5 * math.log2(math.e)

    ql = a[:, :MLA_Q_RANK]
    qn = ql * lax.rsqrt(jnp.mean(ql * ql, axis=-1, keepdims=True) + EPS) * gq_ref[...]
    q = jnp.dot(qn.astype(BF16), wq_ref[...], preferred_element_type=F32)
    cos8 = jnp.concatenate([cos] * MLA_HEADS, axis=-1)
    sin8 = jnp.concatenate([sin] * MLA_HEADS, axis=-1)
    q = (q * cos8 + _swap_rope_halves(q) * sin8) * scale
    q_ref[0] = q.astype(q_ref.dtype)

    kvl = a[:, MLA_Q_RANK:MLA_Q_RANK + MLA_KV_RANK]
    kvn = kvl * lax.rsqrt(jnp.mean(kvl * kvl, axis=-1, keepdims=True) + EPS) * gkv_ref[...]
    kvb = kvn.astype(BF16)
    kn = jnp.dot(kvb, wk_ref[...], preferred_element_type=F32)
    v_ref[0] = jnp.dot(kvb, wv_ref[...], preferred_element_type=F32).astype(v_ref.dtype)

    blk = a[:, MLA_Q_RANK + MLA_KV_RANK:]
    kr = jnp.where(_lane_iota(blk.shape) < MLA_ROPE, blk, 0.0)
    kr = pltpu.roll(kr, ROPE_LANE, 1)
    kr = kr * cos + _swap_rope_halves(kr) * sin
    k = kn + jnp.concatenate([kr] * MLA_HEADS, axis=-1)
    k_ref[0] = k.astype(k_ref.dtype)


def _mla_proj(a_mla, gq, gkv, wq, wk, wv, cos, sin, tm):
    b, seq, _ = a_mla.shape
    hw = MLA_HEADS * HEAD_PAD
    vw = MLA_HEADS * MLA_V
    const = lambda i, t: (0, 0)
    return pl.pallas_call(
        _mla_proj_kernel,
        grid=(b, seq // tm),
        in_specs=[
            pl.BlockSpec((1, tm, SEG), lambda i, t: (i, t, 0)),
            pl.BlockSpec((1, MLA_Q_RANK), const),
            pl.BlockSpec((1, MLA_KV_RANK), const),
            pl.BlockSpec((MLA_Q_RANK, hw), const),
            pl.BlockSpec((MLA_KV_RANK, hw), const),
            pl.BlockSpec((MLA_KV_RANK, vw), const),
            pl.BlockSpec((tm, HEAD_PAD), lambda i, t: (t, 0)),
            pl.BlockSpec((tm, HEAD_PAD), lambda i, t: (t, 0)),
        ],
        out_specs=[
            pl.BlockSpec((1, tm, hw), lambda i, t: (i, t, 0)),
            pl.BlockSpec((1, tm, hw), lambda i, t: (i, t, 0)),
            pl.BlockSpec((1, tm, vw), lambda i, t: (i, t, 0)),
        ],
        out_shape=[
            jax.ShapeDtypeStruct((b, seq, hw), BF16),
            jax.ShapeDtypeStruct((b, seq, hw), BF16),
            jax.ShapeDtypeStruct((b, seq, vw), BF16),
        ],
        compiler_params=_cparams(("parallel", "parallel")),
        name="mla_proj",
    )(a_mla, gq, gkv, wq, wk, wv, cos, sin)


def _attn_kernel(*refs, tk, n_kx):
    if n_kx:
        q_ref, kc_ref, vc_ref, kx_ref, vx_ref, o_ref = refs
    else:
        q_ref, kc_ref, vc_ref, o_ref = refs
    tq = q_ref.shape[1]
    nt_dims = (((1,), (1,)), ((), ()))
    qs = [q_ref[0, :, hh * HEAD_PAD:(hh + 1) * HEAD_PAD] for hh in range(2)]

    def step(q, k, v, carry):
        m, l, acc = carry
        s = lax.dot_general(q, k, nt_dims, preferred_element_type=F32)
        m_new = jnp.maximum(m, jnp.max(s, axis=-1, keepdims=True))
        p = jnp.exp2(s - m_new)
        alpha = jnp.exp2(m - m_new)
        l = alpha * l + jnp.sum(p, axis=-1, keepdims=True)
        acc = alpha * acc + jnp.dot(p.astype(BF16), v, preferred_element_type=F32)
        return m_new, l, acc

    def both_heads(k2, v2, carry):
        return tuple(step(qs[hh], k2[:, hh * HEAD_PAD:(hh + 1) * HEAD_PAD], v2, carry[hh])
                     for hh in range(2))

    init = (jnp.full((tq, 1), -jnp.inf, F32), jnp.zeros((tq, 1), F32),
            jnp.zeros((tq, 2 * MLA_V), F32))
    carry = both_heads(kc_ref[0], vc_ref[0], (init, init))
    if n_kx:
        def body(i, c):
            rows = pl.ds(pl.multiple_of(i * tk, tk), tk)
            return both_heads(kx_ref[0, rows, :], vx_ref[0, rows, :], c)
        carry = lax.fori_loop(0, n_kx, body, carry, unroll=True)
    outs = [acc / l for _, l, acc in carry]
    lane = _lane_iota(outs[0].shape)
    o_ref[0] = jnp.where(lane < MLA_V, outs[0], outs[1]).astype(o_ref.dtype)


def _attention(q, kc, vc, kx, vx, tq, tk):
    b, lq, _ = q.shape
    lc = kc.shape[1]
    n_pairs = MLA_HEADS // 2
    in_specs = [
        pl.BlockSpec((1, tq, 2 * HEAD_PAD), lambda i, p, t: (i, t, p)),
        pl.BlockSpec((1, lc, 2 * HEAD_PAD), lambda i, p, t: (i, 0, p)),
        pl.BlockSpec((1, lc, 2 * MLA_V), lambda i, p, t: (i, 0, p)),
    ]
    args = [q, kc, vc]
    n_kx = 0
    if kx is not None:
        lx = kx.shape[1]
        n_kx = lx // tk
        in_specs += [
            pl.BlockSpec((1, lx, 2 * HEAD_PAD), lambda i, p, t: (i, 0, p)),
            pl.BlockSpec((1, lx, 2 * MLA_V), lambda i, p, t: (i, 0, p)),
        ]
        args += [kx, vx]
    return pl.pallas_call(
        functools.partial(_attn_kernel, tk=tk, n_kx=n_kx),
        grid=(b, n_pairs, lq // tq),
        in_specs=in_specs,
        out_specs=pl.BlockSpec((1, tq, 2 * MLA_V), lambda i, p, t: (i, t, p)),
        out_shape=jax.ShapeDtypeStruct((b, lq, MLA_HEADS * MLA_V), BF16),
        compiler_params=_cparams(("parallel", "parallel", "arbitrary")),
        name="attention",
    )(*args)


def _halo_specs(seq, tm, halo, width):
    r = tm // halo
    last = seq // halo - 1
    prev = pl.BlockSpec((1, halo, width), lambda i, t: (i, jnp.maximum(t * r - 1, 0), 0))
    nxt = pl.BlockSpec((1, halo, width), lambda i, t: (i, jnp.minimum((t + 1) * r, last), 0))
    return prev, nxt


def _conv3(ext_ref, halo, tm, w, b):
    return (b + w[0:1] * ext_ref[pl.ds(halo - 1, tm), :]
            + w[1:2] * ext_ref[pl.ds(halo, tm), :]
            + w[2:3] * ext_ref[pl.ds(halo + 1, tm), :])


def _fill_ext(ext_ref, prev, cur, nxt, halo, tm):
    t = pl.program_id(1)
    first = t == 0
    last = t == pl.num_programs(1) - 1
    ext_ref[0:halo, :] = jnp.where(first, 0.0, prev)
    ext_ref[halo:halo + tm, :] = cur
    ext_ref[halo + tm:, :] = jnp.where(last, 0.0, nxt)


def _log_sigmoid(x):
    return jnp.minimum(x, 0.0) - jnp.log1p(jnp.exp(-jnp.abs(x)))


def _ml_feat_kernel(pp_ref, p_ref, pn_ref, v_ref, g_ref, cw_ref, cb_ref, wq_ref, wk_ref, gb_ref,
                    k_ref, qt_ref, vt_ref, pc_ref, pr_ref, ext_ref, *, halo):
    tm = p_ref.shape[1]
    n_chunk = tm // ML_CHUNK
    _fill_ext(ext_ref, pp_ref[0].astype(F32), p_ref[0].astype(F32), pn_ref[0].astype(F32),
              halo, tm)
    u = jax.nn.silu(_conv3(ext_ref, halo, tm, cw_ref[...], cb_ref[...])).astype(BF16)
    k_ref[0] = jnp.dot(u, wk_ref[...], preferred_element_type=F32).astype(k_ref.dtype)
    q_t = jnp.dot(u, wq_ref[...], preferred_element_type=F32).T
    v_t = v_ref[0].astype(F32).T

    x = g_ref[0] + gb_ref[...]
    lane = _lane_iota(x.shape) - GATE_LANE
    is_gate = (lane >= 0) & (lane < 4 * ML_HEADS)
    is_forget = is_gate & ((lane // ML_HEADS) % 2 == 1)
    packed = jnp.where(is_forget, _log_sigmoid(x), jnp.where(is_gate, x, 0.0))
    lane_c = _lane_iota((ML_CHUNK, LANES)) - GATE_LANE
    fwd_lane = (lane_c >= ML_HEADS) & (lane_c < 2 * ML_HEADS)
    bwd_lane = (lane_c >= 3 * ML_HEADS) & (lane_c < 4 * ML_HEADS)
    r_idx = lax.broadcasted_iota(jnp.int32, (ML_CHUNK, ML_CHUNK), 0)
    c_idx = lax.broadcasted_iota(jnp.int32, (ML_CHUNK, ML_CHUNK), 1)
    lower = (c_idx <= r_idx).astype(F32)
    upper = (c_idx >= r_idx).astype(F32)
    for j in range(n_chunk):
        cols = slice(j * ML_CHUNK, (j + 1) * ML_CHUNK)
        blk = packed[cols, :]
        prefix = jnp.dot(lower, blk, precision=HIGHEST, preferred_element_type=F32)
        suffix = jnp.dot(upper, blk, precision=HIGHEST, preferred_element_type=F32)
        blk = jnp.where(fwd_lane, prefix, jnp.where(bwd_lane, suffix, blk))
        pc_ref[0, cols, :] = blk
        pr_ref[0, j] = blk.T
        qt_ref[0, j] = q_t[:, cols].astype(qt_ref.dtype)
        vt_ref[0, j] = v_t[:, cols].astype(vt_ref.dtype)


def _ml_features(a_ml, a_mla, cw, cb, wq_bd, wk_bd, gate_b, tm):
    b, seq, _ = a_ml.shape
    halo = 16
    w = ML_WIDTH
    n_chunk = tm // ML_CHUNK
    prev, nxt = _halo_specs(seq, tm, halo, w)
    const = lambda i, t: (0, 0)
    gate_block = (SEG - LANES) // LANES
    chunked = lambda rows: pl.BlockSpec((1, n_chunk, rows, ML_CHUNK), lambda i, t: (i, t, 0, 0))
    return pl.pallas_call(
        functools.partial(_ml_feat_kernel, halo=halo),
        grid=(b, seq // tm),
        in_specs=[
            prev,
            pl.BlockSpec((1, tm, w), lambda i, t: (i, t, 0)),
            nxt,
            pl.BlockSpec((1, tm, w), lambda i, t: (i, t, 1)),
            pl.BlockSpec((1, tm, LANES), lambda i, t: (i, t, gate_block)),
            pl.BlockSpec((3, w), const),
            pl.BlockSpec((1, w), const),
            pl.BlockSpec((w, w), const),
            pl.BlockSpec((w, w), const),
            pl.BlockSpec((1, LANES), const),
        ],
        out_specs=[
            pl.BlockSpec((1, tm, w), lambda i, t: (i, t, 0)),
            chunked(w),
            chunked(w),
            pl.BlockSpec((1, tm, LANES), lambda i, t: (i, t, 0)),
            chunked(LANES),
        ],
        out_shape=[
            jax.ShapeDtypeStruct((b, seq, w), BF16),
            jax.ShapeDtypeStruct((b, seq // ML_CHUNK, w, ML_CHUNK), BF16),
            jax.ShapeDtypeStruct((b, seq // ML_CHUNK, w, ML_CHUNK), BF16),
            jax.ShapeDtypeStruct((b, seq, LANES), F32),
            jax.ShapeDtypeStruct((b, seq // ML_CHUNK, LANES, ML_CHUNK), F32),
        ],
        scratch_shapes=[pltpu.VMEM((tm + 2 * halo, w), F32)],
        compiler_params=_cparams(("parallel", "arbitrary")),
        name="ml_features",
    )(a_ml, a_ml, a_ml, a_ml, a_mla, cw, cb, wq_bd, wk_bd, gate_b)


def _ml_chunk(k, q_t, v_t, p_col, p_row, state_ref, m_ref, direction):
    t_len = k.shape[0]
    half = ML_HEAD_DIM
    s_idx = lax.broadcasted_iota(jnp.int32, (t_len, t_len), 0)
    t_idx = lax.broadcasted_iota(jnp.int32, (t_len, t_len), 1)
    keep = (s_idx <= t_idx) if direction == 0 else (s_idx >= t_idx)
    row_p = lax.broadcasted_iota(jnp.int32, (2 * half, t_len), 0)
    last = t_len - 1 if direction == 0 else 0
    outs = []
    for h in range(ML_HEADS):
        pair, odd = divmod(h, 2)
        sl = slice(pair * 2 * half, (pair + 1) * 2 * half)
        mine = (row_p >= half) if odd else (row_p < half)
        one_row = 0 if odd else half
        li_l = GATE_LANE + 2 * direction * ML_HEADS + h
        cu_l = li_l + ML_HEADS
        b_c = p_col[:, li_l:li_l + 1] - p_col[:, cu_l:cu_l + 1]
        li_r = p_row[li_l:li_l + 1, :]
        cu_r = p_row[cu_l:cu_l + 1, :]
        m = m_ref[h]
        state = state_ref[h]

        b_m = jnp.where(keep, b_c, -jnp.inf)
        m_out = cu_r + jnp.maximum(m, jnp.max(b_m, axis=0, keepdims=True))
        kh = k[:, sl]
        qh_t = jnp.where(mine, q_t[sl, :], 0).astype(BF16)
        vh_t = jnp.where(mine, v_t[sl, :].astype(F32), (row_p == one_row).astype(F32))
        s_t = jnp.dot(kh, qh_t, preferred_element_type=F32) * jnp.exp(b_m + (cu_r - m_out))
        a = jnp.exp(cu_r + m - m_out)
        sv = jnp.dot(vh_t.astype(BF16), s_t.astype(BF16), preferred_element_type=F32)
        qc = jnp.dot(state.astype(BF16), qh_t, preferred_element_type=F32)
        den = jnp.sum(s_t, axis=0, keepdims=True) + a * qc[one_row:one_row + 1, :]
        outs.append((sv + a * qc) / jnp.maximum(jnp.abs(den), jnp.exp(-m_out)))

        cum_last = cu_r[:, last:last + 1]
        g = cum_last - cu_r + li_r
        m_new = jnp.maximum(cum_last + m, jnp.max(g, axis=1, keepdims=True))
        wts = jnp.exp(g - m_new)
        decay = jnp.exp(cum_last + m - m_new)
        upd = jnp.dot((vh_t * wts).astype(BF16), kh, preferred_element_type=F32)
        state_ref[h] = decay * state + upd
        m_ref[h] = m_new
    pairs = []
    for pair in range(ML_HEADS // 2):
        pairs.append(jnp.where(row_p < half, outs[2 * pair], outs[2 * pair + 1]))
    return jnp.concatenate(pairs, axis=0).T


def _ml_scan_kernel(kc_ref, qtc_ref, vtc_ref, pcc_ref, prc_ref, oc_ref,
                    kx_ref, qtx_ref, vtx_ref, pcx_ref, prx_ref, ox_ref, ng_ref, bd_ref,
                    hx_ref, hc_ref, fx_scr, bx_scr, fc_scr, bc_scr, state_scr, m_scr):
    t_len = ML_CHUNK
    n_c = kc_ref.shape[1] // t_len
    n_x = kx_ref.shape[1] // t_len
    state_scr[...] = jnp.zeros_like(state_scr)
    m_scr[...] = jnp.zeros_like(m_scr)

    def rows_of(i):
        return pl.ds(pl.multiple_of(i * t_len, t_len), t_len)

    def scan(refs, scrs, n):
        k_ref, qt_ref, vt_ref, pc_ref, pr_ref = refs

        def body(j, _):
            for direction, i in ((0, j), (1, n - 1 - j)):
                rows = rows_of(i)
                scrs[direction][rows, :] = _ml_chunk(
                    k_ref[0, rows, :], qt_ref[0, i], vt_ref[0, i], pc_ref[0, rows, :],
                    pr_ref[0, i], state_scr.at[direction], m_scr.at[direction], direction)
            return 0

        lax.fori_loop(0, n, body, 0)

    def finish(scrs, o_ref, out_ref, n):
        def body(i, _):
            rows = rows_of(i)
            y = (scrs[0][rows, :] + scrs[1][rows, :]) * jax.nn.sigmoid(o_ref[0, rows, :].astype(F32))
            ms = jnp.dot(y * y, bd_ref[...], precision=HIGHEST, preferred_element_type=F32)
            out_ref[0, rows, :] = (y * lax.rsqrt(ms + EPS) * ng_ref[...]).astype(out_ref.dtype)
            return 0

        lax.fori_loop(0, n, body, 0)

    scan((kc_ref, qtc_ref, vtc_ref, pcc_ref, prc_ref), (fc_scr, bc_scr), n_c)
    scan((kx_ref, qtx_ref, vtx_ref, pcx_ref, prx_ref), (fx_scr, bx_scr), n_x)
    finish((fc_scr, bc_scr), oc_ref, hc_ref, n_c)
    finish((fx_scr, bx_scr), ox_ref, hx_ref, n_x)


def _ml_scan(feat_c, a_ml_c, feat_x, a_ml_x, norm_g, bd):
    b, lc, w = feat_c[0].shape
    lx = feat_x[0].shape[1]

    def seq_specs(n):
        n_chunk = n // ML_CHUNK
        return [
            pl.BlockSpec((1, n, w), lambda i: (i, 0, 0)),
            pl.BlockSpec((1, n_chunk, w, ML_CHUNK), lambda i: (i, 0, 0, 0)),
            pl.BlockSpec((1, n_chunk, w, ML_CHUNK), lambda i: (i, 0, 0, 0)),
            pl.BlockSpec((1, n, LANES), lambda i: (i, 0, 0)),
            pl.BlockSpec((1, n_chunk, LANES, ML_CHUNK), lambda i: (i, 0, 0, 0)),
            pl.BlockSpec((1, n, w), lambda i: (i, 0, 2)),
        ]

    const = lambda i: (0, 0)
    return pl.pallas_call(
        _ml_scan_kernel,
        grid=(b,),
        in_specs=seq_specs(lc) + seq_specs(lx) + [
            pl.BlockSpec((1, w), const),
            pl.BlockSpec((w, w), const),
        ],
        out_specs=[
            pl.BlockSpec((1, lx, w), lambda i: (i, 0, 0)),
            pl.BlockSpec((1, lc, w), lambda i: (i, 0, 0)),
        ],
        out_shape=[
            jax.ShapeDtypeStruct((b, lx, w), BF16),
            jax.ShapeDtypeStruct((b, lc, w), BF16),
        ],
        scratch_shapes=[
            pltpu.VMEM((lx, w), F32),
            pltpu.VMEM((lx, w), F32),
            pltpu.VMEM((lc, w), F32),
            pltpu.VMEM((lc, w), F32),
            pltpu.VMEM((2, ML_HEADS, 2 * ML_HEAD_DIM, 2 * ML_HEAD_DIM), F32),
            pltpu.VMEM((2, ML_HEADS, 1, 1), F32),
        ],
        compiler_params=_cparams(("parallel",)),
        name="ml_scan",
    )(*feat_c, a_ml_c, *feat_x, a_ml_x, norm_g, bd)


def _hy_pre_kernel(pp_ref, p_ref, pn_ref, cw_ref, cb_ref, x0_ref, z_ref, ext_ref, *, halo):
    tm = p_ref.shape[1]
    _fill_ext(ext_ref, pp_ref[0].astype(F32), p_ref[0].astype(F32), pn_ref[0].astype(F32),
              halo, tm)
    u = _conv3(ext_ref, halo, tm, cw_ref[...], cb_ref[...])
    x0_ref[0] = u[:, :HY_WIDTH].astype(x0_ref.dtype)
    z_ref[...] = (u[:, HY_WIDTH:2 * HY_WIDTH] * u[:, 2 * HY_WIDTH:]).astype(z_ref.dtype)


def _hy_pre(a_hy, cw, cb, tm):
    b, seq, width = a_hy.shape
    halo = 16
    prev, nxt = _halo_specs(seq, tm, halo, width)
    const = lambda i, t: (0, 0)
    return pl.pallas_call(
        functools.partial(_hy_pre_kernel, halo=halo),
        grid=(b, seq // tm),
        in_specs=[
            prev,
            pl.BlockSpec((1, tm, width), lambda i, t: (i, t, 0)),
            nxt,
            pl.BlockSpec((3, width), const),
            pl.BlockSpec((1, width), const),
        ],
        out_specs=[
            pl.BlockSpec((1, tm, HY_WIDTH), lambda i, t: (i, t, 0)),
            pl.BlockSpec((tm, HY_WIDTH), lambda i, t: (t, i)),
        ],
        out_shape=[
            jax.ShapeDtypeStruct((b, seq, HY_WIDTH), BF16),
            jax.ShapeDtypeStruct((seq, b * HY_WIDTH), BF16),
        ],
        scratch_shapes=[pltpu.VMEM((tm + 2 * halo, width), F32)],
        compiler_params=_cparams(("parallel", "arbitrary")),
        name="hy_pre",
    )(a_hy, a_hy, a_hy, cw, cb)


@functools.lru_cache(maxsize=None)
def _dft_mats(seq, kt):
    n_fft = 2 * seq
    k = np.arange(seq, dtype=np.int64)[:, None]
    n = np.arange(seq, dtype=np.int64)[None, :]
    ang = (2.0 * np.pi / n_fft) * ((k * n) % n_fft).astype(np.float64)
    cos = np.cos(ang)
    msin = -np.sin(ang)
    msin[0, :] = 1.0 - 2.0 * (np.arange(seq) % 2)
    fwd = np.stack([cos.reshape(seq // kt, kt, seq), msin.reshape(seq // kt, kt, seq)], axis=1)
    fwd = fwd.reshape(2 * seq, seq)
    wgt = np.full((seq, 1), 2.0 / n_fft)
    wgt[0, 0] = 1.0 / n_fft
    inv = np.stack([(cos * wgt).reshape(seq // kt, kt, seq),
                    (msin * wgt).reshape(seq // kt, kt, seq)], axis=1)
    inv = inv.reshape(2 * seq, seq).T
    return jnp.asarray(fwd, dtype=BF16), jnp.asarray(np.ascontiguousarray(inv), dtype=BF16)


def _dft_fwd_kernel(f_ref, z_ref, *rest, kt, reps):
    zz = jnp.dot(f_ref[...], z_ref[...], preferred_element_type=F32)
    if not rest[1:]:
        rest[0][...] = zz
        return
    hre_ref, him_ref, hre2_ref, y_ref = rest
    tile = lambda r: jnp.concatenate([r[...]] * reps, axis=-1)
    hre, him, hre2 = tile(hre_ref), tile(him_ref), tile(hre2_ref)
    zre, zim = zz[:kt], zz[kt:]
    y_ref[:kt, :] = (zre * hre - zim * him).astype(y_ref.dtype)
    y_ref[kt:, :] = (zre * him + zim * hre2).astype(y_ref.dtype)


def _dft_fwd(fwd, z, spectrum, kt, cb):
    seq = z.shape[0]
    ncol = z.shape[1]
    in_specs = [
        pl.BlockSpec((2 * kt, seq), lambda i, j: (i, 0)),
        pl.BlockSpec((seq, cb), lambda i, j: (0, j)),
    ]
    args = [fwd, z]
    out_dtype = F32
    if spectrum is not None:
        in_specs += [pl.BlockSpec((kt, HY_WIDTH), lambda i, j: (i, 0))] * 3
        args += list(spectrum)
        out_dtype = BF16
    return pl.pallas_call(
        functools.partial(_dft_fwd_kernel, kt=kt, reps=cb // HY_WIDTH),
        grid=(seq // kt, ncol // cb),
        in_specs=in_specs,
        out_specs=pl.BlockSpec((2 * kt, cb), lambda i, j: (i, j)),
        out_shape=jax.ShapeDtypeStruct((2 * seq, ncol), out_dtype),
        compiler_params=_cparams(("parallel", "arbitrary")),
        name="dft_fwd",
    )(*args)


def _dft_inv_kernel(g_ref, y_ref, z_ref, x0_ref, bias_ref, o_ref, *, nb):
    y = jnp.dot(g_ref[...], y_ref[...], preferred_element_type=F32)
    bias = bias_ref[...]
    for bb in range(nb):
        cols = slice(bb * HY_WIDTH, (bb + 1) * HY_WIDTH)
        yb = y[:, cols] + z_ref[:, cols].astype(F32) * bias
        o_ref[bb] = (x0_ref[bb].astype(F32) * yb).astype(o_ref.dtype)


def _dft_inv(inv, y, z, x0, bias, tm, nb):
    b, seq, _ = x0.shape
    cb = nb * HY_WIDTH
    return pl.pallas_call(
        functools.partial(_dft_inv_kernel, nb=nb),
        grid=(seq // tm, b // nb),
        in_specs=[
            pl.BlockSpec((tm, 2 * seq), lambda t, j: (t, 0)),
            pl.BlockSpec((2 * seq, cb), lambda t, j: (0, j)),
            pl.BlockSpec((tm, cb), lambda t, j: (t, j)),
            pl.BlockSpec((nb, tm, HY_WIDTH), lambda t, j: (j, t, 0)),
            pl.BlockSpec((1, HY_WIDTH), lambda t, j: (0, 0)),
        ],
        out_specs=pl.BlockSpec((nb, tm, HY_WIDTH), lambda t, j: (j, t, 0)),
        out_shape=jax.ShapeDtypeStruct((b, seq, HY_WIDTH), BF16),
        compiler_params=_cparams(("parallel", "arbitrary")),
        name="dft_inv",
    )(inv, y, z, x0, bias)


def _hyena_filter_taps(seq, w1, b1, w2, b2, w3, sin_freq):
    t = jnp.linspace(0.0, 1.0, seq, dtype=F32)[:, None]
    omega = 2.0 * math.pi * jnp.arange(seq, dtype=F32) / seq
    bands = jnp.linspace(1e-4, HY_BANDS - 1, HY_BANDS, dtype=F32)
    ang = omega[:, None] * bands[None, :]
    z = jnp.concatenate([t, jnp.cos(ang), -jnp.sin(ang)], axis=-1)
    hdn = jnp.sin(sin_freq * (jnp.dot(z, w1, precision=HIGHEST) + b1))
    hdn = jnp.sin(sin_freq * (jnp.dot(hdn, w2, precision=HIGHEST) + b2))
    filt = jnp.dot(hdn, w3, precision=HIGHEST)
    deltas = jnp.abs(jnp.linspace(math.log(HY_DECAY_TARGET) / HY_SLOW_PCT,
                                  math.log(HY_DECAY_TARGET) / HY_FAST_PCT, HY_WIDTH, dtype=F32))
    window = jnp.exp(-t * deltas) + HY_SHIFT
    h_fwd = filt[:, :HY_WIDTH] * window
    h_bwd = (filt[:, HY_WIDTH:] * window).at[0].set(0.0)
    l1 = jnp.sum(jnp.abs(h_fwd), axis=0) + jnp.sum(jnp.abs(h_bwd), axis=0)
    return h_fwd, h_bwd, l1


def _hyena_spectrum(seq, kt, fwd, filt_w):
    h_fwd, h_bwd, l1 = _hyena_filter_taps(seq, *filt_w)
    taps = jnp.concatenate([h_fwd, h_bwd], axis=-1).astype(BF16)
    raw = _dft_fwd(fwd, taps, None, kt, 2 * HY_WIDTH)
    raw = raw.reshape(seq // kt, 2, kt, 2 * HY_WIDTH)
    hc = raw[:, 0].reshape(seq, 2 * HY_WIDTH)
    hs = raw[:, 1].reshape(seq, 2 * HY_WIDTH)
    inv_l1 = 1.0 / l1
    hre = (hc[:, :HY_WIDTH] + hc[:, HY_WIDTH:]) * inv_l1
    him = ((hs[:, :HY_WIDTH] - hs[:, HY_WIDTH:]) * inv_l1).at[0].set(0.0)
    nyq = (hs[0, :HY_WIDTH] + hs[0, HY_WIDTH:]) * inv_l1
    hre2 = hre.at[0].set(nyq)
    return hre, him, hre2


def _hyena(a_hy, cw, cb, filt_w, bias_d, tm, kt, nb):
    seq = a_hy.shape[1]
    fwd, inv = _dft_mats(seq, kt)
    x0, z = _hy_pre(a_hy, cw, cb, tm)
    spectrum = _hyena_spectrum(seq, kt, fwd, filt_w)
    y = _dft_fwd(fwd, z, spectrum, kt, nb * HY_WIDTH)
    return _dft_inv(inv, y, z, x0, bias_d, tm, nb)


def _out_proj_kernel(x_ref, att_ref, ml_ref, hy_ref, w_ref, g_ref, o_ref):
    wa = MLA_HEADS * MLA_V
    acc = jnp.dot(att_ref[0], w_ref[:wa, :], preferred_element_type=F32)
    acc += jnp.dot(ml_ref[0], w_ref[wa:wa + ML_WIDTH, :], preferred_element_type=F32)
    acc += jnp.dot(hy_ref[0], w_ref[wa + ML_WIDTH:, :], preferred_element_type=F32)
    o_ref[0] = x_ref[0] + g_ref[0] * acc


def _out_proj(x, att, ml, hy, w, gate, tm):
    b, seq, d = x.shape
    bm = gate.shape[0]
    mod_map = (lambda i, t: (i, 0, 0)) if bm > 1 else (lambda i, t: (0, 0, 0))
    tok = lambda width: pl.BlockSpec((1, tm, width), lambda i, t: (i, t, 0))
    return pl.pallas_call(
        _out_proj_kernel,
        grid=(b, seq // tm),
        in_specs=[tok(d), tok(MLA_HEADS * MLA_V), tok(ML_WIDTH), tok(HY_WIDTH),
                  pl.BlockSpec((d, d), lambda i, t: (0, 0)),
                  pl.BlockSpec((1, 1, d), mod_map)],
        out_specs=tok(d),
        out_shape=jax.ShapeDtypeStruct((b, seq, d), F32),
        compiler_params=_cparams(("parallel", "parallel")),
        name="out_proj",
    )(x, att, ml, hy, w, gate)


def _ffn_kernel(xp_ref, x_ref, xn_ref, g_ref, sc_ref, sh_ref, gate_ref, wup_ref, cw_ref,
                cb_ref, wdn_ref, fg_ref, o_ref, xe_scr, u_scr, acc_scr, *, halo, final_norm):
    tm = x_ref.shape[1]
    n_chunks = wup_ref.shape[0]
    t = pl.program_id(1)

    def norm_mod(x):
        ms = jnp.mean(x * x, axis=-1, keepdims=True)
        return x * lax.rsqrt(ms + EPS) * g_ref[...] * (1.0 + sc_ref[0]) + sh_ref[0]

    prev = jnp.where(t == 0, 0.0, norm_mod(xp_ref[0]))
    nxt = jnp.where(t == pl.num_programs(1) - 1, 0.0, norm_mod(xn_ref[0]))
    xe = jnp.concatenate([prev, norm_mod(x_ref[0]), nxt], axis=0)
    xe_scr[...] = xe.astype(BF16)

    def up_proj(c):
        return jnp.dot(xe_scr[...], wup_ref[c], preferred_element_type=F32)

    u_scr[0] = up_proj(0)
    for c in range(n_chunks):
        slot = c % 2
        if c + 1 < n_chunks:
            u_scr[1 - slot] = up_proj(c + 1)
        uc = _conv3(u_scr.at[slot], halo, tm, cw_ref[c], cb_ref[c])
        act = jax.nn.silu(uc[:, :FF_CHUNK]) * uc[:, FF_CHUNK:]
        down = jnp.dot(act.astype(BF16), wdn_ref[c], preferred_element_type=F32)
        if c == 0:
            acc_scr[...] = down
        else:
            acc_scr[...] += down
    y = x_ref[0] + gate_ref[0] * acc_scr[...]
    if final_norm:
        y = y * lax.rsqrt(jnp.mean(y * y, axis=-1, keepdims=True) + EPS) * fg_ref[...]
    o_ref[0] = y


def _conv_ffn(x, g, sc, sh, gate, wup, cw, cb, wdn, final_g, tm, final_norm):
    b, seq, d = x.shape
    halo = 8
    bm = sc.shape[0]
    mod_map = (lambda i, t: (i, 0, 0)) if bm > 1 else (lambda i, t: (0, 0, 0))
    prev, nxt = _halo_specs(seq, tm, halo, d)
    const2 = lambda i, t: (0, 0)
    const3 = lambda i, t: (0, 0, 0)
    n_chunks = wup.shape[0]
    return pl.pallas_call(
        functools.partial(_ffn_kernel, halo=halo, final_norm=final_norm),
        grid=(b, seq // tm),
        in_specs=[
            prev,
            pl.BlockSpec((1, tm, d), lambda i, t: (i, t, 0)),
            nxt,
            pl.BlockSpec((1, d), const2),
            pl.BlockSpec((1, 1, d), mod_map),
            pl.BlockSpec((1, 1, d), mod_map),
            pl.BlockSpec((1, 1, d), mod_map),
            pl.BlockSpec((n_chunks, d, 2 * FF_CHUNK), const3),
            pl.BlockSpec((n_chunks, 3, 2 * FF_CHUNK), const3),
            pl.BlockSpec((n_chunks, 1, 2 * FF_CHUNK), const3),
            pl.BlockSpec((n_chunks, FF_CHUNK, d), const3),
            pl.BlockSpec((1, d), const2),
        ],
        out_specs=pl.BlockSpec((1, tm, d), lambda i, t: (i, t, 0)),
        out_shape=jax.ShapeDtypeStruct((b, seq, d), F32),
        scratch_shapes=[
            pltpu.VMEM((tm + 2 * halo, d), BF16),
            pltpu.VMEM((2, tm + 2 * halo, 2 * FF_CHUNK), F32),
            pltpu.VMEM((tm, d), F32),
        ],
        compiler_params=_cparams(("parallel", "arbitrary")),
        name="conv_ffn",
    )(x, x, x, g, sc, sh, gate, wup, cw, cb, wdn, final_g)


def _layout_w_in(w):
    d = w.shape[0]
    ml_lo = N_MLA_IN
    hy_lo = N_MLA_IN + N_ML_IN
    gates = w[:, ml_lo + 3 * ML_WIDTH:hy_lo]
    pad = jnp.zeros((d, LANES - MLA_ROPE - 4 * ML_HEADS), w.dtype)
    return jnp.concatenate([w[:, :N_MLA_IN], gates, pad,
                            w[:, ml_lo:ml_lo + 3 * ML_WIDTH], w[:, hy_lo:]], axis=1).astype(BF16)


def _layout_w_uq(w):
    r = w.shape[0]
    w = w.reshape(r, MLA_HEADS, MLA_NOPE + MLA_ROPE)
    pad = jnp.zeros((r, MLA_HEADS, HEAD_PAD - MLA_NOPE - MLA_ROPE), w.dtype)
    return jnp.concatenate([w, pad], axis=-1).reshape(r, MLA_HEADS * HEAD_PAD).astype(BF16)


def _layout_w_ukv(w):
    r = w.shape[0]
    w = w.reshape(r, MLA_HEADS, MLA_NOPE + MLA_V)
    pad = jnp.zeros((r, MLA_HEADS, HEAD_PAD - MLA_NOPE), w.dtype)
    wk = jnp.concatenate([w[..., :MLA_NOPE], pad], axis=-1).reshape(r, MLA_HEADS * HEAD_PAD)
    wv = w[..., MLA_NOPE:].reshape(r, MLA_HEADS * MLA_V)
    return wk.astype(BF16), wv.astype(BF16)


def _block_diag(w):
    h, d, _ = w.shape
    eye = jnp.eye(h, dtype=w.dtype)
    return jnp.einsum('hde,hg->hdge', w, eye).reshape(h * d, h * d)


def _layout_ffn(w_up, conv_w, conv_b, w_down):
    d = w_up.shape[0]
    n = D_FF // FF_CHUNK

    def pair(a):
        lead = a.shape[:-1]
        a = a.reshape(lead + (2, n, FF_CHUNK))
        a = jnp.moveaxis(a, -3, -2)
        return a.reshape(lead + (n, 2 * FF_CHUNK))

    wup = jnp.moveaxis(pair(w_up), 1, 0).astype(BF16)
    cw = jnp.moveaxis(pair(conv_w), 1, 0)
    cb = pair(conv_b)[:, None, :]
    wdn = w_down.reshape(n, FF_CHUNK, d).astype(BF16)
    return wup, cw, cb, wdn


def _rope_tables(n_rows, rope):
    n_freq = MLA_ROPE // 4
    lanes_one = jnp.ones((n_rows * GRID_W, ROPE_LANE), F32)
    lanes_zero = jnp.zeros((n_rows * GRID_W, ROPE_LANE), F32)
    tail = jnp.zeros((n_rows * GRID_W, HEAD_PAD - ROPE_LANE - MLA_ROPE), F32)
    if not rope:
        ones = jnp.ones((n_rows * GRID_W, MLA_ROPE), F32)
        return (jnp.concatenate([lanes_one, ones, tail], axis=-1),
                jnp.concatenate([lanes_zero, 0.0 * ones, tail], axis=-1))
    inv = ROPE_THETA ** (-jnp.arange(n_freq, dtype=F32) / n_freq)
    row = jnp.repeat(jnp.arange(n_rows, dtype=F32), GRID_W)
    col = jnp.tile(jnp.arange(GRID_W, dtype=F32), n_rows)
    ang = jnp.concatenate([row[:, None] * inv, col[:, None] * inv], axis=-1)
    cos, sin = jnp.cos(ang), jnp.sin(ang)
    return (jnp.concatenate([lanes_one, cos, cos, tail], axis=-1),
            jnp.concatenate([lanes_zero, -sin, sin, tail], axis=-1))


def kernel(x, c, ctx, c_ctx, ada_w, ada_b, norm1_g, norm2_g, w_in, mla_q_norm_g, mla_kv_norm_g,
           mla_w_uq, mla_w_ukv, ml_conv_w, ml_conv_b, ml_wq, ml_wk, ml_gate_b, ml_norm_g,
           hy_conv_w, hy_conv_b, hy_w1, hy_b1, hy_w2, hy_b2, hy_w3, hy_sin_freq, hy_bias_d,
           w_out, ffn_w_up, ffn_conv_w, ffn_conv_b, ffn_w_down, final_norm_g):
    b, seq, d = x.shape
    lc = ctx.shape[1]
    depth = ada_w.shape[0]
    tmx = min(512, seq)
    tmc = min(256, lc)
    cos_x, sin_x = _rope_tables(seq // GRID_W, True)
    cos_c, sin_c = _rope_tables(lc // GRID_W, False)
    head_mean = _block_diag(jnp.full((ML_HEADS, ML_HEAD_DIM, ML_HEAD_DIM), 1.0 / ML_HEAD_DIM, F32))
    row = lambda v: v.reshape(1, -1)

    for i in range(depth):
        last = i == depth - 1
        mod_x = jnp.dot(jax.nn.silu(c), ada_w[i], precision=HIGHEST) + ada_b[i]
        mod_c = jnp.dot(jax.nn.silu(c_ctx), ada_w[i], precision=HIGHEST) + ada_b[i]
        mx = [m[:, None, :] for m in jnp.split(mod_x, 6, axis=-1)]
        mc = [m[None, None, :] for m in jnp.split(mod_c, 6, axis=-1)]

        w_in_l = _layout_w_in(w_in[i])
        wq = _layout_w_uq(mla_w_uq[i])
        wk, wv = _layout_w_ukv(mla_w_ukv[i])
        wq_bd = _block_diag(ml_wq[i]).astype(BF16)
        wk_bd = (_block_diag(ml_wk[i]) * (ML_HEAD_DIM ** -0.5)).astype(BF16)
        gate_b = jnp.zeros((1, LANES), F32).at[0, GATE_LANE:GATE_LANE + 4 * ML_HEADS].set(
            ml_gate_b[i].reshape(-1))
        wup, fcw, fcb, wdn = _layout_ffn(ffn_w_up[i], ffn_conv_w[i], ffn_conv_b[i], ffn_w_down[i])
        w_out_l = w_out[i].astype(BF16)
        filt_w = (hy_w1[i], hy_b1[i], hy_w2[i], hy_b2[i], hy_w3[i], hy_sin_freq[i])
        g1 = row(norm1_g[i])

        ax_mla, ax_ml, ax_hy = _in_proj(x, g1, mx[1], mx[0], w_in_l, tmx)
        ac_mla, ac_ml, ac_hy = _in_proj(ctx, g1, mc[1], mc[0], w_in_l, tmc)

        gq, gkv = row(mla_q_norm_g[i]), row(mla_kv_norm_g[i])
        q_x, k_x, v_x = _mla_proj(ax_mla, gq, gkv, wq, wk, wv, cos_x, sin_x, tmx)
        q_c, k_c, v_c = _mla_proj(ac_mla, gq, gkv, wq, wk, wv, cos_c, sin_c, tmc)
        att_x = _attention(q_x, k_c, v_c, k_x, v_x, tmx, min(512, seq))

        mcw, mcb = ml_conv_w[i], row(ml_conv_b[i])
        feat_x = _ml_features(ax_ml, ax_mla, mcw, mcb, wq_bd, wk_bd, gate_b, tmx)
        feat_c = _ml_features(ac_ml, ac_mla, mcw, mcb, wq_bd, wk_bd, gate_b, tmc)
        ml_x, ml_c = _ml_scan(feat_c, ac_ml, feat_x, ax_ml, row(ml_norm_g[i]), head_mean)

        hcw, hcb, hbias = hy_conv_w[i], row(hy_conv_b[i]), row(hy_bias_d[i])
        hy_x = _hyena(ax_hy, hcw, hcb, filt_w, hbias, tmx, min(512, seq), 2)

        x = _out_proj(x, att_x, ml_x, hy_x, w_out_l, mx[2], tmx)
        x = _conv_ffn(x, row(norm2_g[i]), mx[4], mx[3], mx[5], wup, fcw, fcb, wdn,
                      row(final_norm_g), tmx, last)

        if not last:
            att_c = _attention(q_c, k_c, v_c, None, None, tmc, tmc)
            hy_c = _hyena(ac_hy, hcw, hcb, filt_w, hbias, tmc, tmc, 2)
            ctx = _out_proj(ctx, att_c, ml_c, hy_c, w_out_l, mc[2], tmc)
            ctx = _conv_ffn(ctx, row(norm2_g[i]), mc[4], mc[3], mc[5], wup, fcw, fcb, wdn,
                            row(final_norm_g), tmc, False)
    return x
```

```python
import functools
import math

import numpy as np
import jax
import jax.numpy as jnp
from jax import lax
from jax.experimental import pallas as pl
from jax.experimental.pallas import tpu as pltpu

F32 = jnp.float32
BF16 = jnp.bfloat16
HIGHEST = lax.Precision.HIGHEST

D_MODEL = 1024
DEPTH = 2
GRID_W = 64
EPS = 1e-6
MLA_HEADS = 8
MLA_NOPE = 64
MLA_ROPE = 32
MLA_V = 64
MLA_Q_RANK = 384
MLA_KV_RANK = 256
ROPE_THETA = 10000.0
ML_HEADS = 4
ML_HEAD_DIM = 64
ML_WIDTH = 256
HY_WIDTH = 256
HY_BANDS = 16
HY_DECAY_TARGET = 1e-2
HY_FAST_PCT = 0.3
HY_SLOW_PCT = 1.5
HY_SHIFT = 0.05
N_MLA_IN = MLA_Q_RANK + MLA_KV_RANK + MLA_ROPE
N_ML_IN = 3 * ML_WIDTH + 4 * ML_HEADS
D_FF = 2816

LANES = 128
HEAD_PAD = 128
SEG = 768
ROPE_LANE = 64
GATE_LANE = 32
ML_CHUNK = 128
FF_CHUNK = 256
VMEM_LIMIT = 56 * 1024 * 1024


def _cparams(sem):
    return pltpu.CompilerParams(dimension_semantics=sem, vmem_limit_bytes=VMEM_LIMIT)


def _lane_iota(shape):
    return lax.broadcasted_iota(jnp.int32, shape, len(shape) - 1)


def _in_proj_kernel(x_ref, g_ref, sc_ref, sh_ref, w_ref, o_mla, o_ml, o_hy):
    x = x_ref[0]
    ms = jnp.mean(x * x, axis=-1, keepdims=True)
    h = x * lax.rsqrt(ms + EPS) * g_ref[...]
    hb = (h * (1.0 + sc_ref[0]) + sh_ref[0]).astype(BF16)
    for i, o_ref in enumerate((o_mla, o_ml, o_hy)):
        y = jnp.dot(hb, w_ref[:, i * SEG:(i + 1) * SEG], preferred_element_type=F32)
        o_ref[0] = y.astype(o_ref.dtype)


def _in_proj(x, g, sc, sh, w, tm):
    b, seq, d = x.shape
    bm = sc.shape[0]
    mod_map = (lambda i, t: (i, 0, 0)) if bm > 1 else (lambda i, t: (0, 0, 0))
    out_spec = pl.BlockSpec((1, tm, SEG), lambda i, t: (i, t, 0))
    return pl.pallas_call(
        _in_proj_kernel,
        grid=(b, seq // tm),
        in_specs=[
            pl.BlockSpec((1, tm, d), lambda i, t: (i, t, 0)),
            pl.BlockSpec((1, d), lambda i, t: (0, 0)),
            pl.BlockSpec((1, 1, d), mod_map),
            pl.BlockSpec((1, 1, d), mod_map),
            pl.BlockSpec((d, 3 * SEG), lambda i, t: (0, 0)),
        ],
        out_specs=[out_spec, out_spec, out_spec],
        out_shape=[
            jax.ShapeDtypeStruct((b, seq, SEG), F32),
            jax.ShapeDtypeStruct((b, seq, SEG), BF16),
            jax.ShapeDtypeStruct((b, seq, SEG), BF16),
        ],
        compiler_params=_cparams(("parallel", "parallel")),
        name="in_proj",
    )(x, g, sc, sh, w)


def _swap_rope_halves(x):
    width = x.shape[-1]
    lane = _lane_iota(x.shape) % HEAD_PAD
    lo = pltpu.roll(x, width - MLA_ROPE // 2, x.ndim - 1)
    hi = pltpu.roll(x, MLA_ROPE // 2, x.ndim - 1)
    return jnp.where(lane < ROPE_LANE + MLA_ROPE // 2, lo, hi)


def _mla_proj_kernel(a_ref, gq_ref, gkv_ref, wq_ref, wk_ref, wv_ref, cos_ref, sin_ref,
                     q_ref, k_ref, v_ref):
    a = a_ref[0]
    cos = cos_ref[...]
    sin = sin_ref[...]
    scale = (MLA_NOPE + MLA_ROPE) ** -0.5 * math.log2(math.e)

    ql = a[:, :MLA_Q_RANK]
    qn = ql * lax.rsqrt(jnp.mean(ql * ql, axis=-1, keepdims=True) + EPS) * gq_ref[...]
    q = jnp.dot(qn.astype(BF16), wq_ref[...], preferred_element_type=F32)
    cos8 = jnp.concatenate([cos] * MLA_HEADS, axis=-1)
    sin8 = jnp.concatenate([sin] * MLA_HEADS, axis=-1)
    q = (q * cos8 + _swap_rope_halves(q) * sin8) * scale
    q_ref[0] = q.astype(q_ref.dtype)

    kvl = a[:, MLA_Q_RANK:MLA_Q_RANK + MLA_KV_RANK]
    kvn = kvl * lax.rsqrt(jnp.mean(kvl * kvl, axis=-1, keepdims=True) + EPS) * gkv_ref[...]
    kvb = kvn.astype(BF16)
    kn = jnp.dot(kvb, wk_ref[...], preferred_element_type=F32)
    v_ref[0] = jnp.dot(kvb, wv_ref[...], preferred_element_type=F32).astype(v_ref.dtype)

    blk = a[:, MLA_Q_RANK + MLA_KV_RANK:]
    kr = jnp.where(_lane_iota(blk.shape) < MLA_ROPE, blk, 0.0)
    kr = pltpu.roll(kr, ROPE_LANE, 1)
    kr = kr * cos + _swap_rope_halves(kr) * sin
    k = kn + jnp.concatenate([kr] * MLA_HEADS, axis=-1)
    k_ref[0] = k.astype(k_ref.dtype)


def _mla_proj(a_mla, gq, gkv, wq, wk, wv, cos, sin, tm):
    b, seq, _ = a_mla.shape
    hw = MLA_HEADS * HEAD_PAD
    vw = MLA_HEADS * MLA_V
    const = lambda i, t: (0, 0)
    return pl.pallas_call(
        _mla_proj_kernel,
        grid=(b, seq // tm),
        in_specs=[
            pl.BlockSpec((1, tm, SEG), lambda i, t: (i, t, 0)),
            pl.BlockSpec((1, MLA_Q_RANK), const),
            pl.BlockSpec((1, MLA_KV_RANK), const),
            pl.BlockSpec((MLA_Q_RANK, hw), const),
            pl.BlockSpec((MLA_KV_RANK, hw), const),
            pl.BlockSpec((MLA_KV_RANK, vw), const),
            pl.BlockSpec((tm, HEAD_PAD), lambda i, t: (t, 0)),
            pl.BlockSpec((tm, HEAD_PAD), lambda i, t: (t, 0)),
        ],
        out_specs=[
            pl.BlockSpec((1, tm, hw), lambda i, t: (i, t, 0)),
            pl.BlockSpec((1, tm, hw), lambda i, t: (i, t, 0)),
            pl.BlockSpec((1, tm, vw), lambda i, t: (i, t, 0)),
        ],
        out_shape=[
            jax.ShapeDtypeStruct((b, seq, hw), BF16),
            jax.ShapeDtypeStruct((b, seq, hw), BF16),
            jax.ShapeDtypeStruct((b, seq, vw), BF16),
        ],
        compiler_params=_cparams(("parallel", "parallel")),
        name="mla_proj",
    )(a_mla, gq, gkv, wq, wk, wv, cos, sin)


def _attn_kernel(*refs, tk, n_kx):
    if n_kx:
        q_ref, kc_ref, vc_ref, kx_ref, vx_ref, o_ref = refs
    else:
        q_ref, kc_ref, vc_ref, o_ref = refs
    tq = q_ref.shape[1]
    nt_dims = (((1,), (1,)), ((), ()))
    qs = [q_ref[0, :, hh * HEAD_PAD:(hh + 1) * HEAD_PAD] for hh in range(2)]

    def step(q, k, v, carry):
        m, l, acc = carry
        s = lax.dot_general(q, k, nt_dims, preferred_element_type=F32)
        m_new = jnp.maximum(m, jnp.max(s, axis=-1, keepdims=True))
        p = jnp.exp2(s - m_new)
        alpha = jnp.exp2(m - m_new)
        part = p[:, :LANES]
        for j in range(1, p.shape[1] // LANES):
            part = part + p[:, j * LANES:(j + 1) * LANES]
        l = alpha * l + part
        acc = alpha * acc + jnp.dot(p.astype(BF16), v, preferred_element_type=F32)
        return m_new, l, acc

    def both_heads(k2, v2, carry):
        return tuple(step(qs[hh], k2[:, hh * HEAD_PAD:(hh + 1) * HEAD_PAD], v2, carry[hh])
                     for hh in range(2))

    init = (jnp.full((tq, 1), -jnp.inf, F32), jnp.zeros((tq, LANES), F32),
            jnp.zeros((tq, 2 * MLA_V), F32))
    carry = both_heads(kc_ref[0], vc_ref[0], (init, init))
    if n_kx:
        def body(i, c):
            rows = pl.ds(pl.multiple_of(i * tk, tk), tk)
            return both_heads(kx_ref[0, rows, :], vx_ref[0, rows, :], c)
        carry = lax.fori_loop(0, n_kx, body, carry, unroll=True)
    outs = [acc / jnp.sum(l, axis=-1, keepdims=True) for _, l, acc in carry]
    lane = _lane_iota(outs[0].shape)
    o_ref[0] = jnp.where(lane < MLA_V, outs[0], outs[1]).astype(o_ref.dtype)


def _attention(q, kc, vc, kx, vx, tq, tk):
    b, lq, _ = q.shape
    lc = kc.shape[1]
    n_pairs = MLA_HEADS // 2
    in_specs = [
        pl.BlockSpec((1, tq, 2 * HEAD_PAD), lambda i, p, t: (i, t, p)),
        pl.BlockSpec((1, lc, 2 * HEAD_PAD), lambda i, p, t: (i, 0, p)),
        pl.BlockSpec((1, lc, 2 * MLA_V), lambda i, p, t: (i, 0, p)),
    ]
    args = [q, kc, vc]
    n_kx = 0
    if kx is not None:
        lx = kx.shape[1]
        n_kx = lx // tk
        in_specs += [
            pl.BlockSpec((1, lx, 2 * HEAD_PAD), lambda i, p, t: (i, 0, p)),
            pl.BlockSpec((1, lx, 2 * MLA_V), lambda i, p, t: (i, 0, p)),
        ]
        args += [kx, vx]
    return pl.pallas_call(
        functools.partial(_attn_kernel, tk=tk, n_kx=n_kx),
        grid=(b, n_pairs, lq // tq),
        in_specs=in_specs,
        out_specs=pl.BlockSpec((1, tq, 2 * MLA_V), lambda i, p, t: (i, t, p)),
        out_shape=jax.ShapeDtypeStruct((b, lq, MLA_HEADS * MLA_V), BF16),
        compiler_params=_cparams(("parallel", "parallel", "arbitrary")),
        name="attention",
    )(*args)


def _halo_specs(seq, tm, halo, width):
    r = tm // halo
    last = seq // halo - 1
    prev = pl.BlockSpec((1, halo, width), lambda i, t: (i, jnp.maximum(t * r - 1, 0), 0))
    nxt = pl.BlockSpec((1, halo, width), lambda i, t: (i, jnp.minimum((t + 1) * r, last), 0))
    return prev, nxt


def _conv3(ext_ref, halo, tm, w, b):
    return (b + w[0:1] * ext_ref[pl.ds(halo - 1, tm), :]
            + w[1:2] * ext_ref[pl.ds(halo, tm), :]
            + w[2:3] * ext_ref[pl.ds(halo + 1, tm), :])


def _fill_ext(ext_ref, prev, cur, nxt, halo, tm):
    t = pl.program_id(1)
    first = t == 0
    last = t == pl.num_programs(1) - 1
    ext_ref[0:halo, :] = jnp.where(first, 0.0, prev)
    ext_ref[halo:halo + tm, :] = cur
    ext_ref[halo + tm:, :] = jnp.where(last, 0.0, nxt)


def _log_sigmoid(x):
    return jnp.minimum(x, 0.0) - jnp.log1p(jnp.exp(-jnp.abs(x)))


def _ml_feat_kernel(pp_ref, p_ref, pn_ref, v_ref, g_ref, cw_ref, cb_ref, wq_ref, wk_ref, gb_ref,
                    k_ref, qt_ref, vt_ref, pc_ref, pr_ref, ext_ref, *, halo):
    tm = p_ref.shape[1]
    n_chunk = tm // ML_CHUNK
    _fill_ext(ext_ref, pp_ref[0].astype(F32), p_ref[0].astype(F32), pn_ref[0].astype(F32),
              halo, tm)
    u = jax.nn.silu(_conv3(ext_ref, halo, tm, cw_ref[...], cb_ref[...])).astype(BF16)
    k_ref[0] = jnp.dot(u, wk_ref[...], preferred_element_type=F32).astype(k_ref.dtype)
    q_t = jnp.dot(u, wq_ref[...], preferred_element_type=F32).T
    v_t = v_ref[0].astype(F32).T

    x = g_ref[0] + gb_ref[...]
    lane = _lane_iota(x.shape) - GATE_LANE
    is_gate = (lane >= 0) & (lane < 4 * ML_HEADS)
    is_forget = is_gate & ((lane // ML_HEADS) % 2 == 1)
    packed = jnp.where(is_forget, _log_sigmoid(x), jnp.where(is_gate, x, 0.0))
    lane_c = _lane_iota((ML_CHUNK, LANES)) - GATE_LANE
    fwd_lane = (lane_c >= ML_HEADS) & (lane_c < 2 * ML_HEADS)
    bwd_lane = (lane_c >= 3 * ML_HEADS) & (lane_c < 4 * ML_HEADS)
    r_idx = lax.broadcasted_iota(jnp.int32, (ML_CHUNK, ML_CHUNK), 0)
    c_idx = lax.broadcasted_iota(jnp.int32, (ML_CHUNK, ML_CHUNK), 1)
    lower = (c_idx <= r_idx).astype(F32)
    upper = (c_idx >= r_idx).astype(F32)
    for j in range(n_chunk):
        cols = slice(j * ML_CHUNK, (j + 1) * ML_CHUNK)
        blk = packed[cols, :]
        prefix = jnp.dot(lower, blk, precision=HIGHEST, preferred_element_type=F32)
        suffix = jnp.dot(upper, blk, precision=HIGHEST, preferred_element_type=F32)
        blk = jnp.where(fwd_lane, prefix, jnp.where(bwd_lane, suffix, blk))
        pc_ref[0, cols, :] = blk
        pr_ref[0, j] = blk.T
        qt_ref[0, j] = q_t[:, cols].astype(qt_ref.dtype)
        vt_ref[0, j] = v_t[:, cols].astype(vt_ref.dtype)


def _ml_features(a_ml, a_mla, cw, cb, wq_bd, wk_bd, gate_b, tm):
    b, seq, _ = a_ml.shape
    halo = 16
    w = ML_WIDTH
    n_chunk = tm // ML_CHUNK
    prev, nxt = _halo_specs(seq, tm, halo, w)
    const = lambda i, t: (0, 0)
    gate_block = (SEG - LANES) // LANES
    chunked = lambda rows: pl.BlockSpec((1, n_chunk, rows, ML_CHUNK), lambda i, t: (i, t, 0, 0))
    return pl.pallas_call(
        functools.partial(_ml_feat_kernel, halo=halo),
        grid=(b, seq // tm),
        in_specs=[
            prev,
            pl.BlockSpec((1, tm, w), lambda i, t: (i, t, 0)),
            nxt,
            pl.BlockSpec((1, tm, w), lambda i, t: (i, t, 1)),
            pl.BlockSpec((1, tm, LANES), lambda i, t: (i, t, gate_block)),
            pl.BlockSpec((3, w), const),
            pl.BlockSpec((1, w), const),
            pl.BlockSpec((w, w), const),
            pl.BlockSpec((w, w), const),
            pl.BlockSpec((1, LANES), const),
        ],
        out_specs=[
            pl.BlockSpec((1, tm, w), lambda i, t: (i, t, 0)),
            chunked(w),
            chunked(w),
            pl.BlockSpec((1, tm, LANES), lambda i, t: (i, t, 0)),
            chunked(LANES),
        ],
        out_shape=[
            jax.ShapeDtypeStruct((b, seq, w), BF16),
            jax.ShapeDtypeStruct((b, seq // ML_CHUNK, w, ML_CHUNK), BF16),
            jax.ShapeDtypeStruct((b, seq // ML_CHUNK, w, ML_CHUNK), BF16),
            jax.ShapeDtypeStruct((b, seq, LANES), F32),
            jax.ShapeDtypeStruct((b, seq // ML_CHUNK, LANES, ML_CHUNK), F32),
        ],
        scratch_shapes=[pltpu.VMEM((tm + 2 * halo, w), F32)],
        compiler_params=_cparams(("parallel", "arbitrary")),
        name="ml_features",
    )(a_ml, a_ml, a_ml, a_ml, a_mla, cw, cb, wq_bd, wk_bd, gate_b)


def _ml_chunk(k, q_t, v_t, p_col, p_row, state_ref, m_ref, direction):
    t_len = k.shape[0]
    half = ML_HEAD_DIM
    s_idx = lax.broadcasted_iota(jnp.int32, (t_len, t_len), 0)
    t_idx = lax.broadcasted_iota(jnp.int32, (t_len, t_len), 1)
    keep = (s_idx <= t_idx) if direction == 0 else (s_idx >= t_idx)
    row_p = lax.broadcasted_iota(jnp.int32, (2 * half, t_len), 0)
    last = t_len - 1 if direction == 0 else 0
    outs = []
    for h in range(ML_HEADS):
        pair, odd = divmod(h, 2)
        sl = slice(pair * 2 * half, (pair + 1) * 2 * half)
        mine = (row_p >= half) if odd else (row_p < half)
        one_row = 0 if odd else half
        li_l = GATE_LANE + 2 * direction * ML_HEADS + h
        cu_l = li_l + ML_HEADS
        b_c = p_col[:, li_l:li_l + 1] - p_col[:, cu_l:cu_l + 1]
        li_r = p_row[li_l:li_l + 1, :]
        cu_r = p_row[cu_l:cu_l + 1, :]
        m = m_ref[h]
        state = state_ref[h]

        b_m = jnp.where(keep, b_c, -jnp.inf)
        m_out = cu_r + jnp.maximum(m, jnp.max(b_m, axis=0, keepdims=True))
        kh = k[:, sl]
        qh_t = jnp.where(mine, q_t[sl, :], 0).astype(BF16)
        vh_t = jnp.where(mine, v_t[sl, :].astype(F32), (row_p == one_row).astype(F32))
        s_t = jnp.dot(kh, qh_t, preferred_element_type=F32) * jnp.exp(b_m + (cu_r - m_out))
        a = jnp.exp(cu_r + m - m_out)
        lhs = jnp.concatenate([vh_t.astype(BF16), state.astype(BF16)], axis=1)
        rhs = jnp.concatenate([s_t.astype(BF16), (qh_t.astype(F32) * a).astype(BF16)], axis=0)
        tot = jnp.dot(lhs, rhs, preferred_element_type=F32)
        den = jnp.maximum(jnp.abs(tot[one_row:one_row + 1, :]), jnp.exp(-m_out))
        outs.append(tot * (1.0 / den))

        cum_last = cu_r[:, last:last + 1]
        g = cum_last - cu_r + li_r
        m_new = jnp.maximum(cum_last + m, jnp.max(g, axis=1, keepdims=True))
        wts = jnp.exp(g - m_new)
        decay = jnp.exp(cum_last + m - m_new)
        upd = jnp.dot((vh_t * wts).astype(BF16), kh, preferred_element_type=F32)
        state_ref[h] = decay * state + upd
        m_ref[h] = m_new
    pairs = []
    for pair in range(ML_HEADS // 2):
        pairs.append(jnp.where(row_p < half, outs[2 * pair], outs[2 * pair + 1]))
    return jnp.concatenate(pairs, axis=0)


def _ml_scan_kernel(kc_ref, qtc_ref, vtc_ref, pcc_ref, prc_ref, oc_ref,
                    kx_ref, qtx_ref, vtx_ref, pcx_ref, prx_ref, ox_ref, ng_ref, bd_ref,
                    hx_ref, hc_ref, fx_scr, bx_scr, fc_scr, bc_scr, state_scr, m_scr):
    t_len = ML_CHUNK
    n_c = kc_ref.shape[1] // t_len
    n_x = kx_ref.shape[1] // t_len
    state_scr[...] = jnp.zeros_like(state_scr)
    m_scr[...] = jnp.zeros_like(m_scr)

    def rows_of(i):
        return pl.ds(pl.multiple_of(i * t_len, t_len), t_len)

    def scan(refs, scrs, n):
        k_ref, qt_ref, vt_ref, pc_ref, pr_ref = refs

        def body(j, _):
            for direction, i in ((0, j), (1, n - 1 - j)):
                rows = rows_of(i)
                scrs[direction][i] = _ml_chunk(
                    k_ref[0, rows, :], qt_ref[0, i], vt_ref[0, i], pc_ref[0, rows, :],
                    pr_ref[0, i], state_scr.at[direction], m_scr.at[direction], direction)
            return 0

        lax.fori_loop(0, n, body, 0, unroll=2)

    def finish(scrs, o_ref, out_ref, n):
        def body(i, _):
            rows = rows_of(i)
            h = (scrs[0][i] + scrs[1][i]).T
            y = h * jax.nn.sigmoid(o_ref[0, rows, :].astype(F32))
            ms = jnp.dot(y * y, bd_ref[...], precision=HIGHEST, preferred_element_type=F32)
            out_ref[0, rows, :] = (y * lax.rsqrt(ms + EPS) * ng_ref[...]).astype(out_ref.dtype)
            return 0

        lax.fori_loop(0, n, body, 0, unroll=2)

    scan((kc_ref, qtc_ref, vtc_ref, pcc_ref, prc_ref), (fc_scr, bc_scr), n_c)
    scan((kx_ref, qtx_ref, vtx_ref, pcx_ref, prx_ref), (fx_scr, bx_scr), n_x)
    finish((fc_scr, bc_scr), oc_ref, hc_ref, n_c)
    finish((fx_scr, bx_scr), ox_ref, hx_ref, n_x)


def _ml_scan(feat_c, a_ml_c, feat_x, a_ml_x, norm_g, bd):
    b, lc, w = feat_c[0].shape
    lx = feat_x[0].shape[1]

    def seq_specs(n):
        n_chunk = n // ML_CHUNK
        return [
            pl.BlockSpec((1, n, w), lambda i: (i, 0, 0)),
            pl.BlockSpec((1, n_chunk, w, ML_CHUNK), lambda i: (i, 0, 0, 0)),
            pl.BlockSpec((1, n_chunk, w, ML_CHUNK), lambda i: (i, 0, 0, 0)),
            pl.BlockSpec((1, n, LANES), lambda i: (i, 0, 0)),
            pl.BlockSpec((1, n_chunk, LANES, ML_CHUNK), lambda i: (i, 0, 0, 0)),
            pl.BlockSpec((1, n, w), lambda i: (i, 0, 2)),
        ]

    const = lambda i: (0, 0)
    return pl.pallas_call(
        _ml_scan_kernel,
        grid=(b,),
        in_specs=seq_specs(lc) + seq_specs(lx) + [
            pl.BlockSpec((1, w), const),
            pl.BlockSpec((w, w), const),
        ],
        out_specs=[
            pl.BlockSpec((1, lx, w), lambda i: (i, 0, 0)),
            pl.BlockSpec((1, lc, w), lambda i: (i, 0, 0)),
        ],
        out_shape=[
            jax.ShapeDtypeStruct((b, lx, w), BF16),
            jax.ShapeDtypeStruct((b, lc, w), BF16),
        ],
        scratch_shapes=[
            pltpu.VMEM((lx // ML_CHUNK, w, ML_CHUNK), F32),
            pltpu.VMEM((lx // ML_CHUNK, w, ML_CHUNK), F32),
            pltpu.VMEM((lc // ML_CHUNK, w, ML_CHUNK), F32),
            pltpu.VMEM((lc // ML_CHUNK, w, ML_CHUNK), F32),
            pltpu.VMEM((2, ML_HEADS, 2 * ML_HEAD_DIM, 2 * ML_HEAD_DIM), F32),
            pltpu.VMEM((2, ML_HEADS, 1, 1), F32),
        ],
        compiler_params=_cparams(("parallel",)),
        name="ml_scan",
    )(*feat_c, a_ml_c, *feat_x, a_ml_x, norm_g, bd)


def _hy_pre_kernel(pp_ref, p_ref, pn_ref, cw_ref, cb_ref, x0_ref, z_ref, ext_ref, *, halo):
    tm = p_ref.shape[1]
    _fill_ext(ext_ref, pp_ref[0].astype(F32), p_ref[0].astype(F32), pn_ref[0].astype(F32),
              halo, tm)
    u = _conv3(ext_ref, halo, tm, cw_ref[...], cb_ref[...])
    x0_ref[0] = u[:, :HY_WIDTH].astype(x0_ref.dtype)
    z_ref[...] = (u[:, HY_WIDTH:2 * HY_WIDTH] * u[:, 2 * HY_WIDTH:]).astype(z_ref.dtype)


def _hy_pre(a_hy, cw, cb, tm):
    b, seq, width = a_hy.shape
    halo = 16
    prev, nxt = _halo_specs(seq, tm, halo, width)
    const = lambda i, t: (0, 0)
    return pl.pallas_call(
        functools.partial(_hy_pre_kernel, halo=halo),
        grid=(b, seq // tm),
        in_specs=[
            prev,
            pl.BlockSpec((1, tm, width), lambda i, t: (i, t, 0)),
            nxt,
            pl.BlockSpec((3, width), const),
            pl.BlockSpec((1, width), const),
        ],
        out_specs=[
            pl.BlockSpec((1, tm, HY_WIDTH), lambda i, t: (i, t, 0)),
            pl.BlockSpec((tm, HY_WIDTH), lambda i, t: (t, i)),
        ],
        out_shape=[
            jax.ShapeDtypeStruct((b, seq, HY_WIDTH), BF16),
            jax.ShapeDtypeStruct((seq, b * HY_WIDTH), BF16),
        ],
        scratch_shapes=[pltpu.VMEM((tm + 2 * halo, width), F32)],
        compiler_params=_cparams(("parallel", "arbitrary")),
        name="hy_pre",
    )(a_hy, a_hy, a_hy, cw, cb)


@functools.lru_cache(maxsize=None)
def _dft_mats(seq, kt):
    n_fft = 2 * seq
    k = np.arange(seq, dtype=np.int64)[:, None]
    n = np.arange(seq, dtype=np.int64)[None, :]
    ang = (2.0 * np.pi / n_fft) * ((k * n) % n_fft).astype(np.float64)
    cos = np.cos(ang)
    msin = -np.sin(ang)
    msin[0, :] = 1.0 - 2.0 * (np.arange(seq) % 2)
    fwd = np.stack([cos.reshape(seq // kt, kt, seq), msin.reshape(seq // kt, kt, seq)], axis=1)
    fwd = fwd.reshape(2 * seq, seq)
    wgt = np.full((seq, 1), 2.0 / n_fft)
    wgt[0, 0] = 1.0 / n_fft
    inv = np.stack([(cos * wgt).reshape(seq // kt, kt, seq),
                    (msin * wgt).reshape(seq // kt, kt, seq)], axis=1)
    inv = inv.reshape(2 * seq, seq).T
    return jnp.asarray(fwd, dtype=BF16), jnp.asarray(np.ascontiguousarray(inv), dtype=BF16)


def _dft_fwd_kernel(f_ref, z_ref, *rest, kt, reps):
    zz = jnp.dot(f_ref[...], z_ref[...], preferred_element_type=F32)
    if not rest[1:]:
        rest[0][...] = zz
        return
    h_ref, l1_ref, y_ref = rest
    w = HY_WIDTH
    inv_l1 = 1.0 / l1_ref[...]
    hc, hs = h_ref[:kt, :], h_ref[kt:, :]
    first = (lax.broadcasted_iota(jnp.int32, (kt, w), 0) == 0) & (pl.program_id(0) == 0)
    hre = (hc[:, :w] + hc[:, w:]) * inv_l1
    him = jnp.where(first, 0.0, (hs[:, :w] - hs[:, w:]) * inv_l1)
    hre2 = jnp.where(first, (hs[:, :w] + hs[:, w:]) * inv_l1, hre)
    tile = lambda r: jnp.concatenate([r] * reps, axis=-1)
    hre, him, hre2 = tile(hre), tile(him), tile(hre2)
    zre, zim = zz[:kt], zz[kt:]
    y_ref[:kt, :] = (zre * hre - zim * him).astype(y_ref.dtype)
    y_ref[kt:, :] = (zre * him + zim * hre2).astype(y_ref.dtype)


def _dft_fwd(fwd, z, spectrum, kt, cb):
    seq = z.shape[0]
    ncol = z.shape[1]
    in_specs = [
        pl.BlockSpec((2 * kt, seq), lambda i, j: (i, 0)),
        pl.BlockSpec((seq, cb), lambda i, j: (0, j)),
    ]
    args = [fwd, z]
    out_dtype = F32
    if spectrum is not None:
        in_specs += [pl.BlockSpec((2 * kt, 2 * HY_WIDTH), lambda i, j: (i, 0)),
                     pl.BlockSpec((1, HY_WIDTH), lambda i, j: (0, 0))]
        args += list(spectrum)
        out_dtype = BF16
    return pl.pallas_call(
        functools.partial(_dft_fwd_kernel, kt=kt, reps=cb // HY_WIDTH),
        grid=(seq // kt, ncol // cb),
        in_specs=in_specs,
        out_specs=pl.BlockSpec((2 * kt, cb), lambda i, j: (i, j)),
        out_shape=jax.ShapeDtypeStruct((2 * seq, ncol), out_dtype),
        compiler_params=_cparams(("parallel", "arbitrary")),
        name="dft_fwd",
    )(*args)


def _dft_inv_kernel(g_ref, y_ref, z_ref, x0_ref, bias_ref, o_ref, *, nb):
    y = jnp.dot(g_ref[...], y_ref[...], preferred_element_type=F32)
    bias = bias_ref[...]
    for bb in range(nb):
        cols = slice(bb * HY_WIDTH, (bb + 1) * HY_WIDTH)
        yb = y[:, cols] + z_ref[:, cols].astype(F32) * bias
        o_ref[bb] = (x0_ref[bb].astype(F32) * yb).astype(o_ref.dtype)


def _dft_inv(inv, y, z, x0, bias, tm, nb):
    b, seq, _ = x0.shape
    cb = nb * HY_WIDTH
    return pl.pallas_call(
        functools.partial(_dft_inv_kernel, nb=nb),
        grid=(seq // tm, b // nb),
        in_specs=[
            pl.BlockSpec((tm, 2 * seq), lambda t, j: (t, 0)),
            pl.BlockSpec((2 * seq, cb), lambda t, j: (0, j)),
            pl.BlockSpec((tm, cb), lambda t, j: (t, j)),
            pl.BlockSpec((nb, tm, HY_WIDTH), lambda t, j: (j, t, 0)),
            pl.BlockSpec((1, HY_WIDTH), lambda t, j: (0, 0)),
        ],
        out_specs=pl.BlockSpec((nb, tm, HY_WIDTH), lambda t, j: (j, t, 0)),
        out_shape=jax.ShapeDtypeStruct((b, seq, HY_WIDTH), BF16),
        compiler_params=_cparams(("parallel", "arbitrary")),
        name="dft_inv",
    )(inv, y, z, x0, bias)


@functools.lru_cache(maxsize=None)
def _filter_features(seq):
    t = np.linspace(0.0, 1.0, seq)[:, None]
    omega = 2.0 * math.pi * np.arange(seq) / seq
    bands = np.linspace(1e-4, HY_BANDS - 1, HY_BANDS)
    ang = omega[:, None] * bands[None, :]
    feats = np.zeros((seq, LANES), np.float32)
    feats[:, :1 + 2 * HY_BANDS] = np.concatenate([t, np.cos(ang), -np.sin(ang)], axis=-1)
    deltas = np.abs(np.linspace(math.log(HY_DECAY_TARGET) / HY_SLOW_PCT,
                                math.log(HY_DECAY_TARGET) / HY_FAST_PCT, HY_WIDTH))
    return jnp.asarray(feats), jnp.asarray(deltas[None, :].astype(np.float32))


def _filter_kernel(z_ref, dl_ref, w1_ref, b1_ref, w2_ref, b2_ref, w3_ref, fr_ref, taps_ref,
                   l1_ref):
    t = pl.program_id(0)
    z = z_ref[...]
    fr = fr_ref[...]
    dot = functools.partial(jnp.dot, precision=HIGHEST, preferred_element_type=F32)
    hdn = jnp.sin(fr * (dot(z, w1_ref[...]) + b1_ref[...]))
    hdn = jnp.sin(fr * (dot(hdn, w2_ref[...]) + b2_ref[...]))
    filt = dot(hdn, w3_ref[...])
    window = jnp.exp(-z[:, 0:1] * dl_ref[...]) + HY_SHIFT
    h_fwd = filt[:, :HY_WIDTH] * window
    h_bwd = filt[:, HY_WIDTH:] * window
    first = (lax.broadcasted_iota(jnp.int32, h_bwd.shape, 0) == 0) & (t == 0)
    h_bwd = jnp.where(first, 0.0, h_bwd)
    taps_ref[...] = jnp.concatenate([h_fwd, h_bwd], axis=-1).astype(taps_ref.dtype)
    part = jnp.sum(jnp.abs(h_fwd) + jnp.abs(h_bwd), axis=0, keepdims=True)

    @pl.when(t == 0)
    def _():
        l1_ref[...] = part

    @pl.when(t > 0)
    def _():
        l1_ref[...] += part


def _hyena_filter(seq, tm, w1, b1, w2, b2, w3, sin_freq):
    feats, deltas = _filter_features(seq)
    hid = w2.shape[0]
    w1p = jnp.zeros((LANES, hid), F32).at[:w1.shape[0]].set(w1)
    const = lambda t: (0, 0)
    row = lambda v: v.reshape(1, -1)
    return pl.pallas_call(
        _filter_kernel,
        grid=(seq // tm,),
        in_specs=[
            pl.BlockSpec((tm, LANES), lambda t: (t, 0)),
            pl.BlockSpec((1, HY_WIDTH), const),
            pl.BlockSpec((LANES, hid), const),
            pl.BlockSpec((1, hid), const),
            pl.BlockSpec((hid, hid), const),
            pl.BlockSpec((1, hid), const),
            pl.BlockSpec((hid, 2 * HY_WIDTH), const),
            pl.BlockSpec((1, hid), const),
        ],
        out_specs=[
            pl.BlockSpec((tm, 2 * HY_WIDTH), lambda t: (t, 0)),
            pl.BlockSpec((1, HY_WIDTH), const),
        ],
        out_shape=[
            jax.ShapeDtypeStruct((seq, 2 * HY_WIDTH), BF16),
            jax.ShapeDtypeStruct((1, HY_WIDTH), F32),
        ],
        compiler_params=_cparams(("arbitrary",)),
        name="hy_filter",
    )(feats, deltas, w1p, row(b1), w2, row(b2), w3, row(sin_freq))


def _hyena(a_hy, cw, cb, filt_w, bias_d, tm, kt, nb):
    seq = a_hy.shape[1]
    fwd, inv = _dft_mats(seq, kt)
    x0, z = _hy_pre(a_hy, cw, cb, tm)
    taps, l1 = _hyena_filter(seq, tm, *filt_w)
    raw = _dft_fwd(fwd, taps, None, kt, 2 * HY_WIDTH)
    y = _dft_fwd(fwd, z, (raw, l1), kt, nb * HY_WIDTH)
    return _dft_inv(inv, y, z, x0, bias_d, tm, nb)


def _out_proj_kernel(x_ref, att_ref, ml_ref, hy_ref, w_ref, g_ref, o_ref):
    wa = MLA_HEADS * MLA_V
    acc = jnp.dot(att_ref[0], w_ref[:wa, :], preferred_element_type=F32)
    acc += jnp.dot(ml_ref[0], w_ref[wa:wa + ML_WIDTH, :], preferred_element_type=F32)
    acc += jnp.dot(hy_ref[0], w_ref[wa + ML_WIDTH:, :], preferred_element_type=F32)
    o_ref[0] = x_ref[0] + g_ref[0] * acc


def _out_proj(x, att, ml, hy, w, gate, tm):
    b, seq, d = x.shape
    bm = gate.shape[0]
    mod_map = (lambda i, t: (i, 0, 0)) if bm > 1 else (lambda i, t: (0, 0, 0))
    tok = lambda width: pl.BlockSpec((1, tm, width), lambda i, t: (i, t, 0))
    return pl.pallas_call(
        _out_proj_kernel,
        grid=(b, seq // tm),
        in_specs=[tok(d), tok(MLA_HEADS * MLA_V), tok(ML_WIDTH), tok(HY_WIDTH),
                  pl.BlockSpec((d, d), lambda i, t: (0, 0)),
                  pl.BlockSpec((1, 1, d), mod_map)],
        out_specs=tok(d),
        out_shape=jax.ShapeDtypeStruct((b, seq, d), F32),
        compiler_params=_cparams(("parallel", "parallel")),
        name="out_proj",
    )(x, att, ml, hy, w, gate)


def _ffn_kernel(xp_ref, x_ref, xn_ref, g_ref, sc_ref, sh_ref, gate_ref, wup_ref, cw_ref,
                cb_ref, wdn_ref, fg_ref, o_ref, xe_scr, u_scr, acc_scr, *, halo, final_norm):
    tm = x_ref.shape[1]
    d_ff = wdn_ref.shape[0]
    n_chunks = d_ff // FF_CHUNK
    t = pl.program_id(1)

    def cols(c):
        return (slice(c * FF_CHUNK, (c + 1) * FF_CHUNK),
                slice(d_ff + c * FF_CHUNK, d_ff + (c + 1) * FF_CHUNK))

    def norm_mod(x):
        ms = jnp.mean(x * x, axis=-1, keepdims=True)
        return x * lax.rsqrt(ms + EPS) * g_ref[...] * (1.0 + sc_ref[0]) + sh_ref[0]

    prev = jnp.where(t == 0, 0.0, norm_mod(xp_ref[0]))
    nxt = jnp.where(t == pl.num_programs(1) - 1, 0.0, norm_mod(xn_ref[0]))
    xe = jnp.concatenate([prev, norm_mod(x_ref[0]), nxt], axis=0)
    xe_scr[...] = xe.astype(BF16)

    def up_proj(c, slot):
        xe_b = xe_scr[...]
        for half, sl in enumerate(cols(c)):
            u_scr[slot, :, half * FF_CHUNK:(half + 1) * FF_CHUNK] = jnp.dot(
                xe_b, wup_ref[:, sl], preferred_element_type=F32)

    up_proj(0, 0)
    for c in range(n_chunks):
        slot = c % 2
        if c + 1 < n_chunks:
            up_proj(c + 1, 1 - slot)
        gate_cols, val_cols = cols(c)
        cw = jnp.concatenate([cw_ref[:, gate_cols], cw_ref[:, val_cols]], axis=-1)
        cb = jnp.concatenate([cb_ref[:, gate_cols], cb_ref[:, val_cols]], axis=-1)
        uc = _conv3(u_scr.at[slot], halo, tm, cw, cb)
        act = jax.nn.silu(uc[:, :FF_CHUNK]) * uc[:, FF_CHUNK:]
        down = jnp.dot(act.astype(BF16), wdn_ref[gate_cols, :], preferred_element_type=F32)
        if c == 0:
            acc_scr[...] = down
        else:
            acc_scr[...] += down
    y = x_ref[0] + gate_ref[0] * acc_scr[...]
    if final_norm:
        y = y * lax.rsqrt(jnp.mean(y * y, axis=-1, keepdims=True) + EPS) * fg_ref[...]
    o_ref[0] = y


def _conv_ffn(x, g, sc, sh, gate, wup, cw, cb, wdn, final_g, tm, final_norm):
    b, seq, d = x.shape
    halo = 8
    bm = sc.shape[0]
    mod_map = (lambda i, t: (i, 0, 0)) if bm > 1 else (lambda i, t: (0, 0, 0))
    prev, nxt = _halo_specs(seq, tm, halo, d)
    const2 = lambda i, t: (0, 0)
    d_ff = wdn.shape[0]
    return pl.pallas_call(
        functools.partial(_ffn_kernel, halo=halo, final_norm=final_norm),
        grid=(b, seq // tm),
        in_specs=[
            prev,
            pl.BlockSpec((1, tm, d), lambda i, t: (i, t, 0)),
            nxt,
            pl.BlockSpec((1, d), const2),
            pl.BlockSpec((1, 1, d), mod_map),
            pl.BlockSpec((1, 1, d), mod_map),
            pl.BlockSpec((1, 1, d), mod_map),
            pl.BlockSpec((d, 2 * d_ff), const2),
            pl.BlockSpec((3, 2 * d_ff), const2),
            pl.BlockSpec((1, 2 * d_ff), const2),
            pl.BlockSpec((d_ff, d), const2),
            pl.BlockSpec((1, d), const2),
        ],
        out_specs=pl.BlockSpec((1, tm, d), lambda i, t: (i, t, 0)),
        out_shape=jax.ShapeDtypeStruct((b, seq, d), F32),
        scratch_shapes=[
            pltpu.VMEM((tm + 2 * halo, d), BF16),
            pltpu.VMEM((2, tm + 2 * halo, 2 * FF_CHUNK), F32),
            pltpu.VMEM((tm, d), F32),
        ],
        compiler_params=_cparams(("parallel", "arbitrary")),
        name="conv_ffn",
    )(x, x, x, g, sc, sh, gate, wup, cw, cb, wdn, final_g)


def _layout_w_in(w):
    d = w.shape[0]
    ml_lo = N_MLA_IN
    hy_lo = N_MLA_IN + N_ML_IN
    gates = w[:, ml_lo + 3 * ML_WIDTH:hy_lo]
    pad = jnp.zeros((d, LANES - MLA_ROPE - 4 * ML_HEADS), w.dtype)
    return jnp.concatenate([w[:, :N_MLA_IN], gates, pad,
                            w[:, ml_lo:ml_lo + 3 * ML_WIDTH], w[:, hy_lo:]], axis=1).astype(BF16)


def _layout_w_uq(w):
    r = w.shape[0]
    w = w.reshape(r, MLA_HEADS, MLA_NOPE + MLA_ROPE)
    pad = jnp.zeros((r, MLA_HEADS, HEAD_PAD - MLA_NOPE - MLA_ROPE), w.dtype)
    return jnp.concatenate([w, pad], axis=-1).reshape(r, MLA_HEADS * HEAD_PAD).astype(BF16)


def _layout_w_ukv(w):
    r = w.shape[0]
    w = w.reshape(r, MLA_HEADS, MLA_NOPE + MLA_V)
    pad = jnp.zeros((r, MLA_HEADS, HEAD_PAD - MLA_NOPE), w.dtype)
    wk = jnp.concatenate([w[..., :MLA_NOPE], pad], axis=-1).reshape(r, MLA_HEADS * HEAD_PAD)
    wv = w[..., MLA_NOPE:].reshape(r, MLA_HEADS * MLA_V)
    return wk.astype(BF16), wv.astype(BF16)


def _block_diag(w):
    h, d, _ = w.shape
    eye = jnp.eye(h, dtype=w.dtype)
    return jnp.einsum('hde,hg->hdge', w, eye).reshape(h * d, h * d)


@functools.lru_cache(maxsize=None)
def _rope_tables(n_rows, rope):
    n_tok = n_rows * GRID_W
    cos_t = np.zeros((n_tok, HEAD_PAD), np.float32)
    sin_t = np.zeros((n_tok, HEAD_PAD), np.float32)
    cos_t[:, :ROPE_LANE + MLA_ROPE] = 1.0
    if rope:
        n_freq = MLA_ROPE // 4
        half = MLA_ROPE // 2
        inv = ROPE_THETA ** (-np.arange(n_freq, dtype=np.float64) / n_freq)
        row = np.repeat(np.arange(n_rows, dtype=np.float64), GRID_W)
        col = np.tile(np.arange(GRID_W, dtype=np.float64), n_rows)
        ang = np.concatenate([row[:, None] * inv, col[:, None] * inv], axis=-1)
        cos_t[:, ROPE_LANE:ROPE_LANE + half] = np.cos(ang)
        cos_t[:, ROPE_LANE + half:ROPE_LANE + MLA_ROPE] = np.cos(ang)
        sin_t[:, ROPE_LANE:ROPE_LANE + half] = -np.sin(ang)
        sin_t[:, ROPE_LANE + half:ROPE_LANE + MLA_ROPE] = np.sin(ang)
    return jnp.asarray(cos_t), jnp.asarray(sin_t)


def _ada_kernel(c_ref, w_ref, b_ref, o_ref):
    act = jax.nn.silu(c_ref[...])
    o_ref[0] = jnp.dot(act, w_ref[0], precision=HIGHEST, preferred_element_type=F32) + b_ref[0]


def _ada_mod(cvec, ada_w, ada_b):
    depth, d, n = ada_w.shape
    rows = cvec.shape[0]
    tn = n // 4
    return pl.pallas_call(
        _ada_kernel,
        grid=(depth, n // tn),
        in_specs=[
            pl.BlockSpec((rows, d), lambda l, j: (0, 0)),
            pl.BlockSpec((1, d, tn), lambda l, j: (l, 0, j)),
            pl.BlockSpec((1, 1, tn), lambda l, j: (l, 0, j)),
        ],
        out_specs=pl.BlockSpec((1, rows, tn), lambda l, j: (l, 0, j)),
        out_shape=jax.ShapeDtypeStruct((depth, rows, n), F32),
        compiler_params=_cparams(("parallel", "parallel")),
        name="ada_mod",
    )(cvec, ada_w, ada_b.reshape(depth, 1, n))


def kernel(x, c, ctx, c_ctx, ada_w, ada_b, norm1_g, norm2_g, w_in, mla_q_norm_g, mla_kv_norm_g,
           mla_w_uq, mla_w_ukv, ml_conv_w, ml_conv_b, ml_wq, ml_wk, ml_gate_b, ml_norm_g,
           hy_conv_w, hy_conv_b, hy_w1, hy_b1, hy_w2, hy_b2, hy_w3, hy_sin_freq, hy_bias_d,
           w_out, ffn_w_up, ffn_conv_w, ffn_conv_b, ffn_w_down, final_norm_g):
    b, seq, d = x.shape
    lc = ctx.shape[1]
    depth = ada_w.shape[0]
    tmx = min(512, seq)
    tmc = min(256, lc)
    cos_x, sin_x = _rope_tables(seq // GRID_W, True)
    cos_c, sin_c = _rope_tables(lc // GRID_W, False)
    head_mean = _block_diag(jnp.full((ML_HEADS, ML_HEAD_DIM, ML_HEAD_DIM), 1.0 / ML_HEAD_DIM, F32))
    row = lambda v: v.reshape(1, -1)

    n_cond = -(-(b + 1) // 8) * 8
    cvec = jnp.concatenate([c, c_ctx[None], jnp.zeros((n_cond - b - 1, d), c.dtype)], axis=0)
    mod = _ada_mod(cvec, ada_w, ada_b)

    for i in range(depth):
        last = i == depth - 1
        mx = [m[:, None, :] for m in jnp.split(mod[i, :b], 6, axis=-1)]
        mc = [m[:, None, :] for m in jnp.split(mod[i, b:b + 1], 6, axis=-1)]

        w_in_l = _layout_w_in(w_in[i])
        wq = _layout_w_uq(mla_w_uq[i])
        wk, wv = _layout_w_ukv(mla_w_ukv[i])
        wq_bd = _block_diag(ml_wq[i]).astype(BF16)
        wk_bd = (_block_diag(ml_wk[i]) * (ML_HEAD_DIM ** -0.5)).astype(BF16)
        gate_b = jnp.zeros((1, LANES), F32).at[0, GATE_LANE:GATE_LANE + 4 * ML_HEADS].set(
            ml_gate_b[i].reshape(-1))
        wup, wdn = ffn_w_up[i].astype(BF16), ffn_w_down[i].astype(BF16)
        fcw, fcb = ffn_conv_w[i], row(ffn_conv_b[i])
        w_out_l = w_out[i].astype(BF16)
        filt_w = (hy_w1[i], hy_b1[i], hy_w2[i], hy_b2[i], hy_w3[i], hy_sin_freq[i])
        g1 = row(norm1_g[i])

        ax_mla, ax_ml, ax_hy = _in_proj(x, g1, mx[1], mx[0], w_in_l, tmx)
        ac_mla, ac_ml, ac_hy = _in_proj(ctx, g1, mc[1], mc[0], w_in_l, tmc)

        gq, gkv = row(mla_q_norm_g[i]), row(mla_kv_norm_g[i])
        q_x, k_x, v_x = _mla_proj(ax_mla, gq, gkv, wq, wk, wv, cos_x, sin_x, tmx)
        q_c, k_c, v_c = _mla_proj(ac_mla, gq, gkv, wq, wk, wv, cos_c, sin_c, tmc)
        att_x = _attention(q_x, k_c, v_c, k_x, v_x, tmx, min(256, seq))

        mcw, mcb = ml_conv_w[i], row(ml_conv_b[i])
        feat_x = _ml_features(ax_ml, ax_mla, mcw, mcb, wq_bd, wk_bd, gate_b, tmx)
        feat_c = _ml_features(ac_ml, ac_mla, mcw, mcb, wq_bd, wk_bd, gate_b, tmc)
        ml_x, ml_c = _ml_scan(feat_c, ac_ml, feat_x, ax_ml, row(ml_norm_g[i]), head_mean)

        hcw, hcb, hbias = hy_conv_w[i], row(hy_conv_b[i]), row(hy_bias_d[i])
        hy_x = _hyena(ax_hy, hcw, hcb, filt_w, hbias, tmx, min(512, seq), 2)

        x = _out_proj(x, att_x, ml_x, hy_x, w_out_l, mx[2], tmx)
        x = _conv_ffn(x, row(norm2_g[i]), mx[4], mx[3], mx[5], wup, fcw, fcb, wdn,
                      row(final_norm_g), tmx, last)

        if not last:
            att_c = _attention(q_c, k_c, v_c, None, None, tmc, tmc)
            hy_c = _hyena(ac_hy, hcw, hcb, filt_w, hbias, tmc, tmc, 2)
            ctx = _out_proj(ctx, att_c, ml_c, hy_c, w_out_l, mc[2], tmc)
            ctx = _conv_ffn(ctx, row(norm2_g[i]), mc[4], mc[3], mc[5], wup, fcw, fcb, wdn,
                            row(final_norm_g), tmc, False)
    return x
```

```python
import functools
import math

import numpy as np
import jax
import jax.numpy as jnp
from jax import lax
from jax.experimental import pallas as pl
from jax.experimental.pallas import tpu as pltpu

F32 = jnp.float32
BF16 = jnp.bfloat16
HIGHEST = lax.Precision.HIGHEST

D_MODEL = 1024
DEPTH = 2
GRID_W = 64
EPS = 1e-6
MLA_HEADS = 8
MLA_NOPE = 64
MLA_ROPE = 32
MLA_V = 64
MLA_Q_RANK = 384
MLA_KV_RANK = 256
ROPE_THETA = 10000.0
ML_HEADS = 4
ML_HEAD_DIM = 64
ML_WIDTH = 256
HY_WIDTH = 256
HY_BANDS = 16
HY_DECAY_TARGET = 1e-2
HY_FAST_PCT = 0.3
HY_SLOW_PCT = 1.5
HY_SHIFT = 0.05
N_MLA_IN = MLA_Q_RANK + MLA_KV_RANK + MLA_ROPE
N_ML_IN = 3 * ML_WIDTH + 4 * ML_HEADS
D_FF = 2816

LANES = 128
HEAD_PAD = 128
SEG = 768
ROPE_LANE = 64
GATE_LANE = 32
ML_CHUNK = 128
FF_CHUNK = 256
VMEM_LIMIT = 56 * 1024 * 1024


def _cparams(sem):
    return pltpu.CompilerParams(dimension_semantics=sem, vmem_limit_bytes=VMEM_LIMIT)


def _lane_iota(shape):
    return lax.broadcasted_iota(jnp.int32, shape, len(shape) - 1)


def _in_proj_kernel(x_ref, g_ref, sc_ref, sh_ref, w_ref, o_mla, o_ml, o_hy):
    x = x_ref[0]
    ms = jnp.mean(x * x, axis=-1, keepdims=True)
    h = x * lax.rsqrt(ms + EPS) * g_ref[...]
    hb = (h * (1.0 + sc_ref[0]) + sh_ref[0]).astype(BF16)
    for i, o_ref in enumerate((o_mla, o_ml, o_hy)):
        y = jnp.dot(hb, w_ref[:, i * SEG:(i + 1) * SEG], preferred_element_type=F32)
        o_ref[0] = y.astype(o_ref.dtype)


def _in_proj(x, g, sc, sh, w, tm):
    b, seq, d = x.shape
    bm = sc.shape[0]
    mod_map = (lambda i, t: (i, 0, 0)) if bm > 1 else (lambda i, t: (0, 0, 0))
    out_spec = pl.BlockSpec((1, tm, SEG), lambda i, t: (i, t, 0))
    return pl.pallas_call(
        _in_proj_kernel,
        grid=(b, seq // tm),
        in_specs=[
            pl.BlockSpec((1, tm, d), lambda i, t: (i, t, 0)),
            pl.BlockSpec((1, d), lambda i, t: (0, 0)),
            pl.BlockSpec((1, 1, d), mod_map),
            pl.BlockSpec((1, 1, d), mod_map),
            pl.BlockSpec((d, 3 * SEG), lambda i, t: (0, 0)),
        ],
        out_specs=[out_spec, out_spec, out_spec],
        out_shape=[
            jax.ShapeDtypeStruct((b, seq, SEG), F32),
            jax.ShapeDtypeStruct((b, seq, SEG), BF16),
            jax.ShapeDtypeStruct((b, seq, SEG), BF16),
        ],
        compiler_params=_cparams(("parallel", "parallel")),
        name="in_proj",
    )(x, g, sc, sh, w)


def _swap_rope_halves(x):
    width = x.shape[-1]
    lane = _lane_iota(x.shape) % HEAD_PAD
    lo = pltpu.roll(x, width - MLA_ROPE // 2, x.ndim - 1)
    hi = pltpu.roll(x, MLA_ROPE // 2, x.ndim - 1)
    return jnp.where(lane < ROPE_LANE + MLA_ROPE // 2, lo, hi)


def _mla_proj_kernel(a_ref, gq_ref, gkv_ref, wq_ref, wk_ref, wv_ref, cos_ref, sin_ref,
                     q_ref, k_ref, v_ref):
    a = a_ref[0]
    cos = cos_ref[...]
    sin = sin_ref[...]
    scale = (MLA_NOPE + MLA_ROPE) ** -0.5 * math.log2(math.e)

    ql = a[:, :MLA_Q_RANK]
    qn = ql * lax.rsqrt(jnp.mean(ql * ql, axis=-1, keepdims=True) + EPS) * gq_ref[...]
    q = jnp.dot(qn.astype(BF16), wq_ref[...], preferred_element_type=F32)
    cos8 = jnp.concatenate([cos] * MLA_HEADS, axis=-1)
    sin8 = jnp.concatenate([sin] * MLA_HEADS, axis=-1)
    q = (q * cos8 + _swap_rope_halves(q) * sin8) * scale
    q_ref[0] = q.astype(q_ref.dtype)

    kvl = a[:, MLA_Q_RANK:MLA_Q_RANK + MLA_KV_RANK]
    kvn = kvl * lax.rsqrt(jnp.mean(kvl * kvl, axis=-1, keepdims=True) + EPS) * gkv_ref[...]
    kvb = kvn.astype(BF16)
    kn = jnp.dot(kvb, wk_ref[...], preferred_element_type=F32)
    v_ref[0] = jnp.dot(kvb, wv_ref[...], preferred_element_type=F32).astype(v_ref.dtype)

    blk = a[:, MLA_Q_RANK + MLA_KV_RANK:]
    kr = jnp.where(_lane_iota(blk.shape) < MLA_ROPE, blk, 0.0)
    kr = pltpu.roll(kr, ROPE_LANE, 1)
    kr = kr * cos + _swap_rope_halves(kr) * sin
    k = kn + jnp.concatenate([kr] * MLA_HEADS, axis=-1)
    k_ref[0] = k.astype(k_ref.dtype)


def _mla_proj(a_mla, gq, gkv, wq, wk, wv, cos, sin, tm):
    b, seq, _ = a_mla.shape
    hw = MLA_HEADS * HEAD_PAD
    vw = MLA_HEADS * MLA_V
    const = lambda i, t: (0, 0)
    return pl.pallas_call(
        _mla_proj_kernel,
        grid=(b, seq // tm),
        in_specs=[
            pl.BlockSpec((1, tm, SEG), lambda i, t: (i, t, 0)),
            pl.BlockSpec((1, MLA_Q_RANK), const),
            pl.BlockSpec((1, MLA_KV_RANK), const),
            pl.BlockSpec((MLA_Q_RANK, hw), const),
            pl.BlockSpec((MLA_KV_RANK, hw), const),
            pl.BlockSpec((MLA_KV_RANK, vw), const),
            pl.BlockSpec((tm, HEAD_PAD), lambda i, t: (t, 0)),
            pl.BlockSpec((tm, HEAD_PAD), lambda i, t: (t, 0)),
        ],
        out_specs=[
            pl.BlockSpec((1, tm, hw), lambda i, t: (i, t, 0)),
            pl.BlockSpec((1, tm, hw), lambda i, t: (i, t, 0)),
            pl.BlockSpec((1, tm, vw), lambda i, t: (i, t, 0)),
        ],
        out_shape=[
            jax.ShapeDtypeStruct((b, seq, hw), BF16),
            jax.ShapeDtypeStruct((b, seq, hw), BF16),
            jax.ShapeDtypeStruct((b, seq, vw), BF16),
        ],
        compiler_params=_cparams(("parallel", "parallel")),
        name="mla_proj",
    )(a_mla, gq, gkv, wq, wk, wv, cos, sin)


def _attn_kernel(*refs, tk, n_kx):
    if n_kx:
        q_ref, kc_ref, vc_ref, kx_ref, vx_ref, o_ref = refs
    else:
        q_ref, kc_ref, vc_ref, o_ref = refs
    tq = q_ref.shape[1]
    nt_dims = (((1,), (1,)), ((), ()))
    qs = [q_ref[0, :, hh * HEAD_PAD:(hh + 1) * HEAD_PAD] for hh in range(2)]

    def step(q, k, v, carry):
        m, l, acc = carry
        s = lax.dot_general(q, k, nt_dims, preferred_element_type=F32)
        m_new = jnp.maximum(m, jnp.max(s, axis=-1, keepdims=True))
        p = jnp.exp2(s - m_new)
        alpha = jnp.exp2(m - m_new)
        part = p[:, :LANES]
        for j in range(1, p.shape[1] // LANES):
            part = part + p[:, j * LANES:(j + 1) * LANES]
        l = alpha * l + part
        acc = alpha * acc + jnp.dot(p.astype(BF16), v, preferred_element_type=F32)
        return m_new, l, acc

    def both_heads(k2, v2, carry):
        return tuple(step(qs[hh], k2[:, hh * HEAD_PAD:(hh + 1) * HEAD_PAD], v2, carry[hh])
                     for hh in range(2))

    init = (jnp.full((tq, 1), -jnp.inf, F32), jnp.zeros((tq, LANES), F32),
            jnp.zeros((tq, 2 * MLA_V), F32))
    carry = both_heads(kc_ref[0], vc_ref[0], (init, init))
    if n_kx:
        def body(i, c):
            rows = pl.ds(pl.multiple_of(i * tk, tk), tk)
            return both_heads(kx_ref[0, rows, :], vx_ref[0, rows, :], c)
        carry = lax.fori_loop(0, n_kx, body, carry, unroll=True)
    outs = [acc / jnp.sum(l, axis=-1, keepdims=True) for _, l, acc in carry]
    lane = _lane_iota(outs[0].shape)
    o_ref[0] = jnp.where(lane < MLA_V, outs[0], outs[1]).astype(o_ref.dtype)


def _attention(q, kc, vc, kx, vx, tq, tk):
    b, lq, _ = q.shape
    lc = kc.shape[1]
    n_pairs = MLA_HEADS // 2
    in_specs = [
        pl.BlockSpec((1, tq, 2 * HEAD_PAD), lambda i, p, t: (i, t, p)),
        pl.BlockSpec((1, lc, 2 * HEAD_PAD), lambda i, p, t: (i, 0, p)),
        pl.BlockSpec((1, lc, 2 * MLA_V), lambda i, p, t: (i, 0, p)),
    ]
    args = [q, kc, vc]
    n_kx = 0
    if kx is not None:
        lx = kx.shape[1]
        n_kx = lx // tk
        in_specs += [
            pl.BlockSpec((1, lx, 2 * HEAD_PAD), lambda i, p, t: (i, 0, p)),
            pl.BlockSpec((1, lx, 2 * MLA_V), lambda i, p, t: (i, 0, p)),
        ]
        args += [kx, vx]
    return pl.pallas_call(
        functools.partial(_attn_kernel, tk=tk, n_kx=n_kx),
        grid=(b, n_pairs, lq // tq),
        in_specs=in_specs,
        out_specs=pl.BlockSpec((1, tq, 2 * MLA_V), lambda i, p, t: (i, t, p)),
        out_shape=jax.ShapeDtypeStruct((b, lq, MLA_HEADS * MLA_V), BF16),
        compiler_params=_cparams(("parallel", "parallel", "arbitrary")),
        name="attention",
    )(*args)


def _halo_specs(seq, tm, halo, width):
    r = tm // halo
    last = seq // halo - 1
    prev = pl.BlockSpec((1, halo, width), lambda i, t: (i, jnp.maximum(t * r - 1, 0), 0))
    nxt = pl.BlockSpec((1, halo, width), lambda i, t: (i, jnp.minimum((t + 1) * r, last), 0))
    return prev, nxt


def _conv3(ext_ref, halo, tm, w, b):
    return (b + w[0:1] * ext_ref[pl.ds(halo - 1, tm), :]
            + w[1:2] * ext_ref[pl.ds(halo, tm), :]
            + w[2:3] * ext_ref[pl.ds(halo + 1, tm), :])


def _fill_ext(ext_ref, prev, cur, nxt, halo, tm):
    t = pl.program_id(1)
    first = t == 0
    last = t == pl.num_programs(1) - 1
    ext_ref[0:halo, :] = jnp.where(first, 0.0, prev)
    ext_ref[halo:halo + tm, :] = cur
    ext_ref[halo + tm:, :] = jnp.where(last, 0.0, nxt)


def _log_sigmoid(x):
    return jnp.minimum(x, 0.0) - jnp.log1p(jnp.exp(-jnp.abs(x)))


def _ml_feat_kernel(pp_ref, p_ref, pn_ref, v_ref, g_ref, cw_ref, cb_ref, wq_ref, wk_ref, gb_ref,
                    k_ref, qt_ref, vt_ref, pc_ref, pr_ref, ext_ref, *, halo):
    tm = p_ref.shape[1]
    n_chunk = tm // ML_CHUNK
    _fill_ext(ext_ref, pp_ref[0].astype(F32), p_ref[0].astype(F32), pn_ref[0].astype(F32),
              halo, tm)
    u = jax.nn.silu(_conv3(ext_ref, halo, tm, cw_ref[...], cb_ref[...])).astype(BF16)
    k_ref[0] = jnp.dot(u, wk_ref[...], preferred_element_type=F32).astype(k_ref.dtype)
    q_t = jnp.dot(u, wq_ref[...], preferred_element_type=F32).T
    v_t = v_ref[0].astype(F32).T

    x = g_ref[0] + gb_ref[...]
    lane = _lane_iota(x.shape) - GATE_LANE
    is_gate = (lane >= 0) & (lane < 4 * ML_HEADS)
    is_forget = is_gate & ((lane // ML_HEADS) % 2 == 1)
    packed = jnp.where(is_forget, _log_sigmoid(x), jnp.where(is_gate, x, 0.0))
    lane_c = _lane_iota((ML_CHUNK, LANES)) - GATE_LANE
    fwd_lane = (lane_c >= ML_HEADS) & (lane_c < 2 * ML_HEADS)
    bwd_lane = (lane_c >= 3 * ML_HEADS) & (lane_c < 4 * ML_HEADS)
    r_idx = lax.broadcasted_iota(jnp.int32, (ML_CHUNK, ML_CHUNK), 0)
    c_idx = lax.broadcasted_iota(jnp.int32, (ML_CHUNK, ML_CHUNK), 1)
    lower = (c_idx <= r_idx).astype(F32)
    upper = (c_idx >= r_idx).astype(F32)
    for j in range(n_chunk):
        cols = slice(j * ML_CHUNK, (j + 1) * ML_CHUNK)
        blk = packed[cols, :]
        prefix = jnp.dot(lower, blk, precision=HIGHEST, preferred_element_type=F32)
        suffix = jnp.dot(upper, blk, precision=HIGHEST, preferred_element_type=F32)
        blk = jnp.where(fwd_lane, prefix, jnp.where(bwd_lane, suffix, blk))
        pc_ref[0, cols, :] = blk
        pr_ref[0, j] = blk.T
        qt_ref[0, j] = q_t[:, cols].astype(qt_ref.dtype)
        vt_ref[0, j] = v_t[:, cols].astype(vt_ref.dtype)


def _ml_features(a_ml, a_mla, cw, cb, wq_bd, wk_bd, gate_b, tm):
    b, seq, _ = a_ml.shape
    halo = 16
    w = ML_WIDTH
    n_chunk = tm // ML_CHUNK
    prev, nxt = _halo_specs(seq, tm, halo, w)
    const = lambda i, t: (0, 0)
    gate_block = (SEG - LANES) // LANES
    chunked = lambda rows: pl.BlockSpec((1, n_chunk, rows, ML_CHUNK), lambda i, t: (i, t, 0, 0))
    return pl.pallas_call(
        functools.partial(_ml_feat_kernel, halo=halo),
        grid=(b, seq // tm),
        in_specs=[
            prev,
            pl.BlockSpec((1, tm, w), lambda i, t: (i, t, 0)),
            nxt,
            pl.BlockSpec((1, tm, w), lambda i, t: (i, t, 1)),
            pl.BlockSpec((1, tm, LANES), lambda i, t: (i, t, gate_block)),
            pl.BlockSpec((3, w), const),
            pl.BlockSpec((1, w), const),
            pl.BlockSpec((w, w), const),
            pl.BlockSpec((w, w), const),
            pl.BlockSpec((1, LANES), const),
        ],
        out_specs=[
            pl.BlockSpec((1, tm, w), lambda i, t: (i, t, 0)),
            chunked(w),
            chunked(w),
            pl.BlockSpec((1, tm, LANES), lambda i, t: (i, t, 0)),
            chunked(LANES),
        ],
        out_shape=[
            jax.ShapeDtypeStruct((b, seq, w), BF16),
            jax.ShapeDtypeStruct((b, seq // ML_CHUNK, w, ML_CHUNK), BF16),
            jax.ShapeDtypeStruct((b, seq // ML_CHUNK, w, ML_CHUNK), BF16),
            jax.ShapeDtypeStruct((b, seq, LANES), F32),
            jax.ShapeDtypeStruct((b, seq // ML_CHUNK, LANES, ML_CHUNK), F32),
        ],
        scratch_shapes=[pltpu.VMEM((tm + 2 * halo, w), F32)],
        compiler_params=_cparams(("parallel", "arbitrary")),
        name="ml_features",
    )(a_ml, a_ml, a_ml, a_ml, a_mla, cw, cb, wq_bd, wk_bd, gate_b)


def _ml_chunk(k, q_t, v_t, p_col, p_row, state_ref, m_ref, direction):
    t_len = k.shape[0]
    half = ML_HEAD_DIM
    s_idx = lax.broadcasted_iota(jnp.int32, (t_len, t_len), 0)
    t_idx = lax.broadcasted_iota(jnp.int32, (t_len, t_len), 1)
    keep = (s_idx <= t_idx) if direction == 0 else (s_idx >= t_idx)
    row_p = lax.broadcasted_iota(jnp.int32, (2 * half, t_len), 0)
    last = t_len - 1 if direction == 0 else 0
    outs = []
    for h in range(ML_HEADS):
        pair, odd = divmod(h, 2)
        sl = slice(pair * 2 * half, (pair + 1) * 2 * half)
        mine = (row_p >= half) if odd else (row_p < half)
        one_row = 0 if odd else half
        li_l = GATE_LANE + 2 * direction * ML_HEADS + h
        cu_l = li_l + ML_HEADS
        b_c = p_col[:, li_l:li_l + 1] - p_col[:, cu_l:cu_l + 1]
        li_r = p_row[li_l:li_l + 1, :]
        cu_r = p_row[cu_l:cu_l + 1, :]
        m = m_ref[h]
        state = state_ref[h]

        b_m = jnp.where(keep, b_c, -jnp.inf)
        m_out = cu_r + jnp.maximum(m, jnp.max(b_m, axis=0, keepdims=True))
        kh = k[:, sl]
        qh_t = jnp.where(mine, q_t[sl, :], 0).astype(BF16)
        vh_t = jnp.where(mine, v_t[sl, :].astype(F32), (row_p == one_row).astype(F32))
        s_t = jnp.dot(kh, qh_t, preferred_element_type=F32) * jnp.exp(b_m + (cu_r - m_out))
        a = jnp.exp(cu_r + m - m_out)
        lhs = jnp.concatenate([vh_t.astype(BF16), state.astype(BF16)], axis=1)
        rhs = jnp.concatenate([s_t.astype(BF16), (qh_t.astype(F32) * a).astype(BF16)], axis=0)
        tot = jnp.dot(lhs, rhs, preferred_element_type=F32)
        den = jnp.maximum(jnp.abs(tot[one_row:one_row + 1, :]), jnp.exp(-m_out))
        outs.append(tot * (1.0 / den))

        cum_last = cu_r[:, last:last + 1]
        g = cum_last - cu_r + li_r
        m_new = jnp.maximum(cum_last + m, jnp.max(g, axis=1, keepdims=True))
        wts = jnp.exp(g - m_new)
        decay = jnp.exp(cum_last + m - m_new)
        upd = jnp.dot((vh_t * wts).astype(BF16), kh, preferred_element_type=F32)
        state_ref[h] = decay * state + upd
        m_ref[h] = m_new
    pairs = []
    for pair in range(ML_HEADS // 2):
        pairs.append(jnp.where(row_p < half, outs[2 * pair], outs[2 * pair + 1]))
    return jnp.concatenate(pairs, axis=0)


def _ml_scan_kernel(kc_ref, qtc_ref, vtc_ref, pcc_ref, prc_ref, oc_ref,
                    kx_ref, qtx_ref, vtx_ref, pcx_ref, prx_ref, ox_ref, ng_ref, bd_ref,
                    hx_ref, hc_ref, fx_scr, bx_scr, fc_scr, bc_scr, state_scr, m_scr):
    t_len = ML_CHUNK
    n_c = kc_ref.shape[1] // t_len
    n_x = kx_ref.shape[1] // t_len
    state_scr[...] = jnp.zeros_like(state_scr)
    m_scr[...] = jnp.zeros_like(m_scr)

    def rows_of(i):
        return pl.ds(pl.multiple_of(i * t_len, t_len), t_len)

    def scan(refs, scrs, n):
        k_ref, qt_ref, vt_ref, pc_ref, pr_ref = refs

        def body(j, _):
            for direction, i in ((0, j), (1, n - 1 - j)):
                rows = rows_of(i)
                scrs[direction][i] = _ml_chunk(
                    k_ref[0, rows, :], qt_ref[0, i], vt_ref[0, i], pc_ref[0, rows, :],
                    pr_ref[0, i], state_scr.at[direction], m_scr.at[direction], direction)
            return 0

        lax.fori_loop(0, n, body, 0, unroll=2)

    def finish(scrs, o_ref, out_ref, n):
        def body(i, _):
            rows = rows_of(i)
            h = (scrs[0][i] + scrs[1][i]).T
            y = h * jax.nn.sigmoid(o_ref[0, rows, :].astype(F32))
            ms = jnp.dot(y * y, bd_ref[...], precision=HIGHEST, preferred_element_type=F32)
            out_ref[0, rows, :] = (y * lax.rsqrt(ms + EPS) * ng_ref[...]).astype(out_ref.dtype)
            return 0

        lax.fori_loop(0, n, body, 0, unroll=2)

    scan((kc_ref, qtc_ref, vtc_ref, pcc_ref, prc_ref), (fc_scr, bc_scr), n_c)
    scan((kx_ref, qtx_ref, vtx_ref, pcx_ref, prx_ref), (fx_scr, bx_scr), n_x)
    finish((fc_scr, bc_scr), oc_ref, hc_ref, n_c)
    finish((fx_scr, bx_scr), ox_ref, hx_ref, n_x)


def _ml_scan(feat_c, a_ml_c, feat_x, a_ml_x, norm_g, bd):
    b, lc, w = feat_c[0].shape
    lx = feat_x[0].shape[1]

    def seq_specs(n):
        n_chunk = n // ML_CHUNK
        return [
            pl.BlockSpec((1, n, w), lambda i: (i, 0, 0)),
            pl.BlockSpec((1, n_chunk, w, ML_CHUNK), lambda i: (i, 0, 0, 0)),
            pl.BlockSpec((1, n_chunk, w, ML_CHUNK), lambda i: (i, 0, 0, 0)),
            pl.BlockSpec((1, n, LANES), lambda i: (i, 0, 0)),
            pl.BlockSpec((1, n_chunk, LANES, ML_CHUNK), lambda i: (i, 0, 0, 0)),
            pl.BlockSpec((1, n, w), lambda i: (i, 0, 2)),
        ]

    const = lambda i: (0, 0)
    return pl.pallas_call(
        _ml_scan_kernel,
        grid=(b,),
        in_specs=seq_specs(lc) + seq_specs(lx) + [
            pl.BlockSpec((1, w), const),
            pl.BlockSpec((w, w), const),
        ],
        out_specs=[
            pl.BlockSpec((1, lx, w), lambda i: (i, 0, 0)),
            pl.BlockSpec((1, lc, w), lambda i: (i, 0, 0)),
        ],
        out_shape=[
            jax.ShapeDtypeStruct((b, lx, w), BF16),
            jax.ShapeDtypeStruct((b, lc, w), BF16),
        ],
        scratch_shapes=[
            pltpu.VMEM((lx // ML_CHUNK, w, ML_CHUNK), F32),
            pltpu.VMEM((lx // ML_CHUNK, w, ML_CHUNK), F32),
            pltpu.VMEM((lc // ML_CHUNK, w, ML_CHUNK), F32),
            pltpu.VMEM((lc // ML_CHUNK, w, ML_CHUNK), F32),
            pltpu.VMEM((2, ML_HEADS, 2 * ML_HEAD_DIM, 2 * ML_HEAD_DIM), F32),
            pltpu.VMEM((2, ML_HEADS, 1, 1), F32),
        ],
        compiler_params=_cparams(("parallel",)),
        name="ml_scan",
    )(*feat_c, a_ml_c, *feat_x, a_ml_x, norm_g, bd)


def _hy_pre_kernel(pp_ref, p_ref, pn_ref, cw_ref, cb_ref, x0_ref, z_ref, ext_ref, *, halo):
    tm = p_ref.shape[1]
    _fill_ext(ext_ref, pp_ref[0].astype(F32), p_ref[0].astype(F32), pn_ref[0].astype(F32),
              halo, tm)
    u = _conv3(ext_ref, halo, tm, cw_ref[...], cb_ref[...])
    x0_ref[0] = u[:, :HY_WIDTH].astype(x0_ref.dtype)
    z_ref[...] = (u[:, HY_WIDTH:2 * HY_WIDTH] * u[:, 2 * HY_WIDTH:]).astype(z_ref.dtype)


def _hy_pre(a_hy, cw, cb, tm):
    b, seq, width = a_hy.shape
    halo = 16
    prev, nxt = _halo_specs(seq, tm, halo, width)
    const = lambda i, t: (0, 0)
    return pl.pallas_call(
        functools.partial(_hy_pre_kernel, halo=halo),
        grid=(b, seq // tm),
        in_specs=[
            prev,
            pl.BlockSpec((1, tm, width), lambda i, t: (i, t, 0)),
            nxt,
            pl.BlockSpec((3, width), const),
            pl.BlockSpec((1, width), const),
        ],
        out_specs=[
            pl.BlockSpec((1, tm, HY_WIDTH), lambda i, t: (i, t, 0)),
            pl.BlockSpec((tm, HY_WIDTH), lambda i, t: (t, i)),
        ],
        out_shape=[
            jax.ShapeDtypeStruct((b, seq, HY_WIDTH), BF16),
            jax.ShapeDtypeStruct((seq, b * HY_WIDTH), BF16),
        ],
        scratch_shapes=[pltpu.VMEM((tm + 2 * halo, width), F32)],
        compiler_params=_cparams(("parallel", "arbitrary")),
        name="hy_pre",
    )(a_hy, a_hy, a_hy, cw, cb)


@functools.lru_cache(maxsize=None)
def _dft_mats(seq, kt):
    n_fft = 2 * seq
    k = np.arange(seq, dtype=np.int64)[:, None]
    n = np.arange(seq, dtype=np.int64)[None, :]
    ang = (2.0 * np.pi / n_fft) * ((k * n) % n_fft).astype(np.float64)
    cos = np.cos(ang)
    msin = -np.sin(ang)
    msin[0, :] = 1.0 - 2.0 * (np.arange(seq) % 2)
    fwd = np.stack([cos.reshape(seq // kt, kt, seq), msin.reshape(seq // kt, kt, seq)], axis=1)
    fwd = fwd.reshape(2 * seq, seq)
    wgt = np.full((seq, 1), 2.0 / n_fft)
    wgt[0, 0] = 1.0 / n_fft
    inv = np.stack([(cos * wgt).reshape(seq // kt, kt, seq),
                    (msin * wgt).reshape(seq // kt, kt, seq)], axis=1)
    inv = inv.reshape(2 * seq, seq).T
    return jnp.asarray(fwd, dtype=BF16), jnp.asarray(np.ascontiguousarray(inv), dtype=BF16)


def _dft_fwd_kernel(f_ref, z_ref, *rest, kt, reps):
    zz = jnp.dot(f_ref[...], z_ref[...], preferred_element_type=F32)
    if not rest[1:]:
        rest[0][...] = zz
        return
    h_ref, l1_ref, y_ref = rest
    w = HY_WIDTH
    inv_l1 = 1.0 / l1_ref[...]
    hc, hs = h_ref[:kt, :], h_ref[kt:, :]
    first = (lax.broadcasted_iota(jnp.int32, (kt, w), 0) == 0) & (pl.program_id(0) == 0)
    hre = (hc[:, :w] + hc[:, w:]) * inv_l1
    him = jnp.where(first, 0.0, (hs[:, :w] - hs[:, w:]) * inv_l1)
    hre2 = jnp.where(first, (hs[:, :w] + hs[:, w:]) * inv_l1, hre)
    tile = lambda r: jnp.concatenate([r] * reps, axis=-1)
    hre, him, hre2 = tile(hre), tile(him), tile(hre2)
    zre, zim = zz[:kt], zz[kt:]
    y_ref[:kt, :] = (zre * hre - zim * him).astype(y_ref.dtype)
    y_ref[kt:, :] = (zre * him + zim * hre2).astype(y_ref.dtype)


def _dft_fwd(fwd, z, spectrum, kt, cb):
    seq = z.shape[0]
    ncol = z.shape[1]
    in_specs = [
        pl.BlockSpec((2 * kt, seq), lambda i, j: (i, 0)),
        pl.BlockSpec((seq, cb), lambda i, j: (0, j)),
    ]
    args = [fwd, z]
    out_dtype = F32
    if spectrum is not None:
        in_specs += [pl.BlockSpec((2 * kt, 2 * HY_WIDTH), lambda i, j: (i, 0)),
                     pl.BlockSpec((1, HY_WIDTH), lambda i, j: (0, 0))]
        args += list(spectrum)
        out_dtype = BF16
    return pl.pallas_call(
        functools.partial(_dft_fwd_kernel, kt=kt, reps=cb // HY_WIDTH),
        grid=(seq // kt, ncol // cb),
        in_specs=in_specs,
        out_specs=pl.BlockSpec((2 * kt, cb), lambda i, j: (i, j)),
        out_shape=jax.ShapeDtypeStruct((2 * seq, ncol), out_dtype),
        compiler_params=_cparams(("parallel", "arbitrary")),
        name="dft_fwd",
    )(*args)


def _dft_inv_kernel(g_ref, y_ref, z_ref, x0_ref, bias_ref, o_ref, *, nb):
    y = jnp.dot(g_ref[...], y_ref[...], preferred_element_type=F32)
    bias = bias_ref[...]
    for bb in range(nb):
        cols = slice(bb * HY_WIDTH, (bb + 1) * HY_WIDTH)
        yb = y[:, cols] + z_ref[:, cols].astype(F32) * bias
        o_ref[bb] = (x0_ref[bb].astype(F32) * yb).astype(o_ref.dtype)


def _dft_inv(inv, y, z, x0, bias, tm, nb):
    b, seq, _ = x0.shape
    cb = nb * HY_WIDTH
    return pl.pallas_call(
        functools.partial(_dft_inv_kernel, nb=nb),
        grid=(seq // tm, b // nb),
        in_specs=[
            pl.BlockSpec((tm, 2 * seq), lambda t, j: (t, 0)),
            pl.BlockSpec((2 * seq, cb), lambda t, j: (0, j)),
            pl.BlockSpec((tm, cb), lambda t, j: (t, j)),
            pl.BlockSpec((nb, tm, HY_WIDTH), lambda t, j: (j, t, 0)),
            pl.BlockSpec((1, HY_WIDTH), lambda t, j: (0, 0)),
        ],
        out_specs=pl.BlockSpec((nb, tm, HY_WIDTH), lambda t, j: (j, t, 0)),
        out_shape=jax.ShapeDtypeStruct((b, seq, HY_WIDTH), BF16),
        compiler_params=_cparams(("parallel", "arbitrary")),
        name="dft_inv",
    )(inv, y, z, x0, bias)


@functools.lru_cache(maxsize=None)
def _filter_features(seq):
    t = np.linspace(0.0, 1.0, seq)[:, None]
    omega = 2.0 * math.pi * np.arange(seq) / seq
    bands = np.linspace(1e-4, HY_BANDS - 1, HY_BANDS)
    ang = omega[:, None] * bands[None, :]
    feats = np.zeros((seq, LANES), np.float32)
    feats[:, :1 + 2 * HY_BANDS] = np.concatenate([t, np.cos(ang), -np.sin(ang)], axis=-1)
    deltas = np.abs(np.linspace(math.log(HY_DECAY_TARGET) / HY_SLOW_PCT,
                                math.log(HY_DECAY_TARGET) / HY_FAST_PCT, HY_WIDTH))
    return jnp.asarray(feats), jnp.asarray(deltas[None, :].astype(np.float32))


def _filter_kernel(z_ref, dl_ref, w1_ref, b1_ref, w2_ref, b2_ref, w3_ref, fr_ref, taps_ref,
                   l1_ref):
    t = pl.program_id(0)
    z = z_ref[...]
    fr = fr_ref[...]
    dot = functools.partial(jnp.dot, precision=HIGHEST, preferred_element_type=F32)
    hdn = jnp.sin(fr * (dot(z, w1_ref[...]) + b1_ref[...]))
    hdn = jnp.sin(fr * (dot(hdn, w2_ref[...]) + b2_ref[...]))
    filt = dot(hdn, w3_ref[...])
    window = jnp.exp(-z[:, 0:1] * dl_ref[...]) + HY_SHIFT
    h_fwd = filt[:, :HY_WIDTH] * window
    h_bwd = filt[:, HY_WIDTH:] * window
    first = (lax.broadcasted_iota(jnp.int32, h_bwd.shape, 0) == 0) & (t == 0)
    h_bwd = jnp.where(first, 0.0, h_bwd)
    taps_ref[...] = jnp.concatenate([h_fwd, h_bwd], axis=-1).astype(taps_ref.dtype)
    part = jnp.sum(jnp.abs(h_fwd) + jnp.abs(h_bwd), axis=0, keepdims=True)

    @pl.when(t == 0)
    def _():
        l1_ref[...] = part

    @pl.when(t > 0)
    def _():
        l1_ref[...] += part


def _hyena_filter(seq, tm, w1, b1, w2, b2, w3, sin_freq):
    feats, deltas = _filter_features(seq)
    hid = w2.shape[0]
    w1p = jnp.zeros((LANES, hid), F32).at[:w1.shape[0]].set(w1)
    const = lambda t: (0, 0)
    row = lambda v: v.reshape(1, -1)
    return pl.pallas_call(
        _filter_kernel,
        grid=(seq // tm,),
        in_specs=[
            pl.BlockSpec((tm, LANES), lambda t: (t, 0)),
            pl.BlockSpec((1, HY_WIDTH), const),
            pl.BlockSpec((LANES, hid), const),
            pl.BlockSpec((1, hid), const),
            pl.BlockSpec((hid, hid), const),
            pl.BlockSpec((1, hid), const),
            pl.BlockSpec((hid, 2 * HY_WIDTH), const),
            pl.BlockSpec((1, hid), const),
        ],
        out_specs=[
            pl.BlockSpec((tm, 2 * HY_WIDTH), lambda t: (t, 0)),
            pl.BlockSpec((1, HY_WIDTH), const),
        ],
        out_shape=[
            jax.ShapeDtypeStruct((seq, 2 * HY_WIDTH), BF16),
            jax.ShapeDtypeStruct((1, HY_WIDTH), F32),
        ],
        compiler_params=_cparams(("arbitrary",)),
        name="hy_filter",
    )(feats, deltas, w1p, row(b1), w2, row(b2), w3, row(sin_freq))


FFT_RADIX = 16


@functools.lru_cache(maxsize=None)
def _fft_consts(seq):
    n_fft = 2 * seq
    s_half = FFT_RADIX // 2
    f_len = n_fft // FFT_RADIX
    ks = np.arange(s_half + 1)
    theta = 2.0 * np.pi * np.outer(ks, np.arange(s_half)) / FFT_RADIX
    wgt = np.where((ks == 0) | (ks == s_half), 1.0, 2.0)[:, None] / n_fft
    coef = np.concatenate([np.cos(theta), -np.sin(theta), wgt * np.cos(theta),
                           -wgt * np.sin(theta)], axis=1).astype(np.float32)
    f = np.arange(f_len, dtype=np.int64)
    fwd, inv = [], []
    for k0 in ks:
        k = k0 + FFT_RADIX * np.arange(f_len, dtype=np.int64)
        ang = (2.0 * np.pi / n_fft) * ((k[:, None] * f[None, :]) % n_fft)
        m_re, m_im = np.cos(ang), -np.sin(ang)
        fwd.append(np.concatenate([m_re, m_im], axis=0))
        inv.append(np.concatenate([m_re.T, m_im.T], axis=0))
    return (jnp.asarray(coef), jnp.asarray(np.stack(fwd), dtype=BF16),
            jnp.asarray(np.stack(inv), dtype=BF16))


def _fft_forward(coef_ref, ks, z_ref, fm_ref, f_len, width):
    s_half = FFT_RADIX // 2
    a_re = a_im = None
    for s in range(s_half):
        blk = z_ref[s * f_len:(s + 1) * f_len, :].astype(F32)
        t_re, t_im = coef_ref[ks, s] * blk, coef_ref[ks, s_half + s] * blk
        a_re = t_re if a_re is None else a_re + t_re
        a_im = t_im if a_im is None else a_im + t_im
    p = jnp.dot(fm_ref[0], jnp.concatenate([a_re, a_im], axis=1).astype(BF16),
                preferred_element_type=F32)
    x_re = p[:f_len, :width] - p[f_len:, width:]
    x_im = p[f_len:, :width] + p[:f_len, width:]
    return x_re, x_im


def _fft_filter_kernel(coef_ref, taps_ref, fm_ref, o_ref, *, f_len):
    x_re, x_im = _fft_forward(coef_ref, pl.program_id(0), taps_ref, fm_ref, f_len,
                              taps_ref.shape[1])
    o_ref[0, :f_len, :] = x_re
    o_ref[0, f_len:, :] = x_im


def _fft_conv_kernel(coef_ref, z_ref, fm_ref, im_ref, h_ref, l1_ref, x0_ref, bias_ref, o_ref,
                     y_scr, *, f_len):
    ks = pl.program_id(1)
    s_half = FFT_RADIX // 2
    w = HY_WIDTH
    x_re, x_im = _fft_forward(coef_ref, ks, z_ref, fm_ref, f_len, w)
    inv_l1 = 1.0 / l1_ref[...]
    h_re = (h_ref[0, :f_len, :w] + h_ref[0, :f_len, w:]) * inv_l1
    h_im = (h_ref[0, f_len:, :w] - h_ref[0, f_len:, w:]) * inv_l1
    y_re = x_re * h_re - x_im * h_im
    y_im = x_re * h_im + x_im * h_re
    r = jnp.dot(im_ref[0], jnp.concatenate([y_re, y_im], axis=1).astype(BF16),
                preferred_element_type=F32)
    b_re = r[:f_len, :w] + r[f_len:, w:]
    b_im = r[:f_len, w:] - r[f_len:, :w]

    @pl.when(ks == 0)
    def _():
        y_scr[...] = jnp.zeros_like(y_scr)

    for s in range(s_half):
        rows = slice(s * f_len, (s + 1) * f_len)
        y_scr[rows, :] += (coef_ref[ks, 2 * s_half + s] * b_re
                           + coef_ref[ks, 3 * s_half + s] * b_im)

    @pl.when(ks == s_half)
    def _():
        y = y_scr[...] + z_ref[...].astype(F32) * bias_ref[...]
        o_ref[0] = (x0_ref[0].astype(F32) * y).astype(o_ref.dtype)


def _hyena_fft(z, x0, taps, l1, bias_d):
    b, seq, w = x0.shape
    coef, fwd_m, inv_m = _fft_consts(seq)
    n_ks = FFT_RADIX // 2 + 1
    f_len = 2 * seq // FFT_RADIX
    smem = pl.BlockSpec(memory_space=pltpu.SMEM)
    h_raw = pl.pallas_call(
        functools.partial(_fft_filter_kernel, f_len=f_len),
        grid=(n_ks,),
        in_specs=[smem,
                  pl.BlockSpec((seq, 2 * w), lambda k: (0, 0)),
                  pl.BlockSpec((1, 2 * f_len, f_len), lambda k: (k, 0, 0))],
        out_specs=pl.BlockSpec((1, 2 * f_len, 2 * w), lambda k: (k, 0, 0)),
        out_shape=jax.ShapeDtypeStruct((n_ks, 2 * f_len, 2 * w), F32),
        compiler_params=_cparams(("parallel",)),
        name="hy_fft_filter",
    )(coef, taps, fwd_m)
    return pl.pallas_call(
        functools.partial(_fft_conv_kernel, f_len=f_len),
        grid=(b, n_ks),
        in_specs=[smem,
                  pl.BlockSpec((seq, w), lambda i, k: (0, i)),
                  pl.BlockSpec((1, 2 * f_len, f_len), lambda i, k: (k, 0, 0)),
                  pl.BlockSpec((1, 2 * f_len, f_len), lambda i, k: (k, 0, 0)),
                  pl.BlockSpec((1, 2 * f_len, 2 * w), lambda i, k: (k, 0, 0)),
                  pl.BlockSpec((1, w), lambda i, k: (0, 0)),
                  pl.BlockSpec((1, seq, w), lambda i, k: (i, 0, 0)),
                  pl.BlockSpec((1, w), lambda i, k: (0, 0))],
        out_specs=pl.BlockSpec((1, seq, w), lambda i, k: (i, 0, 0)),
        out_shape=jax.ShapeDtypeStruct((b, seq, w), BF16),
        scratch_shapes=[pltpu.VMEM((seq, w), F32)],
        compiler_params=_cparams(("parallel", "arbitrary")),
        name="hy_fft_conv",
    )(coef, z, fwd_m, inv_m, h_raw, l1, x0, bias_d)


def _hyena(a_hy, cw, cb, filt_w, bias_d, tm, kt, nb):
    seq = a_hy.shape[1]
    x0, z = _hy_pre(a_hy, cw, cb, tm)
    taps, l1 = _hyena_filter(seq, tm, *filt_w)
    if 2 * seq // FFT_RADIX >= LANES:
        return _hyena_fft(z, x0, taps, l1, bias_d)
    fwd, inv = _dft_mats(seq, kt)
    raw = _dft_fwd(fwd, taps, None, kt, 2 * HY_WIDTH)
    y = _dft_fwd(fwd, z, (raw, l1), kt, nb * HY_WIDTH)
    return _dft_inv(inv, y, z, x0, bias_d, tm, nb)


def _ffn_kernel(xp_ref, x_ref, xn_ref, ap_ref, a_ref, an_ref, mp_ref, m_ref, mn_ref, hp_ref,
                h_ref, hn_ref, wo_ref, g1_ref, g_ref, sc_ref, sh_ref, gate_ref, wup_ref, cw_ref,
                cb_ref, wdn_ref, fg_ref, o_ref, xe_scr, u_scr, acc_scr, *, halo, mix_halo,
                final_norm):
    tm = x_ref.shape[1]
    d_ff = wdn_ref.shape[0]
    n_chunks = d_ff // FF_CHUNK
    t = pl.program_id(1)

    def with_halo(p_ref, c_ref, n_ref):
        prev = p_ref[0].astype(F32)[mix_halo - halo:, :]
        nxt = n_ref[0].astype(F32)[:halo, :]
        return jnp.concatenate([prev, c_ref[0].astype(F32), nxt], axis=0).astype(BF16)

    lo = 0
    mixed = None
    for refs in ((ap_ref, a_ref, an_ref), (mp_ref, m_ref, mn_ref), (hp_ref, h_ref, hn_ref)):
        width = refs[1].shape[2]
        part = jnp.dot(with_halo(*refs), wo_ref[lo:lo + width, :], preferred_element_type=F32)
        mixed = part if mixed is None else mixed + part
        lo += width
    x_ext = jnp.concatenate([xp_ref[0], x_ref[0], xn_ref[0]], axis=0) + g1_ref[0] * mixed

    def cols(c):
        return (slice(c * FF_CHUNK, (c + 1) * FF_CHUNK),
                slice(d_ff + c * FF_CHUNK, d_ff + (c + 1) * FF_CHUNK))

    def norm_mod(x):
        ms = jnp.mean(x * x, axis=-1, keepdims=True)
        return x * lax.rsqrt(ms + EPS) * g_ref[...] * (1.0 + sc_ref[0]) + sh_ref[0]

    row = lax.broadcasted_iota(jnp.int32, (tm + 2 * halo, 1), 0)
    outside = ((row < halo) & (t == 0)) | ((row >= tm + halo) & (t == pl.num_programs(1) - 1))
    xe_scr[...] = jnp.where(outside, 0.0, norm_mod(x_ext)).astype(BF16)
    x_mid = x_ext[halo:halo + tm, :]

    def up_proj(c, slot):
        xe_b = xe_scr[...]
        for half, sl in enumerate(cols(c)):
            u_scr[slot, :, half * FF_CHUNK:(half + 1) * FF_CHUNK] = jnp.dot(
                xe_b, wup_ref[:, sl], preferred_element_type=F32)

    up_proj(0, 0)
    for c in range(n_chunks):
        slot = c % 2
        if c + 1 < n_chunks:
            up_proj(c + 1, 1 - slot)
        gate_cols, val_cols = cols(c)
        cw = jnp.concatenate([cw_ref[:, gate_cols], cw_ref[:, val_cols]], axis=-1)
        cb = jnp.concatenate([cb_ref[:, gate_cols], cb_ref[:, val_cols]], axis=-1)
        uc = _conv3(u_scr.at[slot], halo, tm, cw, cb)
        act = jax.nn.silu(uc[:, :FF_CHUNK]) * uc[:, FF_CHUNK:]
        acc_scr[:, gate_cols] = act.astype(BF16)
    down = jnp.dot(acc_scr[...], wdn_ref[...], preferred_element_type=F32)
    y = x_mid + gate_ref[0] * down
    if final_norm:
        y = y * lax.rsqrt(jnp.mean(y * y, axis=-1, keepdims=True) + EPS) * fg_ref[...]
    o_ref[0] = y


def _out_ffn(x, att, ml, hy, w_out, gate1, g, sc, sh, gate, wup, cw, cb, wdn, final_g, tm,
             final_norm):
    b, seq, d = x.shape
    halo = 8
    mix_halo = 16
    bm = sc.shape[0]
    mod_map = (lambda i, t: (i, 0, 0)) if bm > 1 else (lambda i, t: (0, 0, 0))
    prev, nxt = _halo_specs(seq, tm, halo, d)
    const2 = lambda i, t: (0, 0)
    d_ff = wdn.shape[0]
    mix_specs, mix_args = [], []
    for a in (att, ml, hy):
        width = a.shape[2]
        m_prev, m_nxt = _halo_specs(seq, tm, mix_halo, width)
        mix_specs += [m_prev, pl.BlockSpec((1, tm, width), lambda i, t: (i, t, 0)), m_nxt]
        mix_args += [a, a, a]
    return pl.pallas_call(
        functools.partial(_ffn_kernel, halo=halo, mix_halo=mix_halo, final_norm=final_norm),
        grid=(b, seq // tm),
        in_specs=[
            prev,
            pl.BlockSpec((1, tm, d), lambda i, t: (i, t, 0)),
            nxt,
            *mix_specs,
            pl.BlockSpec((d, d), const2, pipeline_mode=pl.Buffered(1)),
            pl.BlockSpec((1, 1, d), mod_map),
            pl.BlockSpec((1, d), const2),
            pl.BlockSpec((1, 1, d), mod_map),
            pl.BlockSpec((1, 1, d), mod_map),
            pl.BlockSpec((1, 1, d), mod_map),
            pl.BlockSpec((d, 2 * d_ff), const2, pipeline_mode=pl.Buffered(1)),
            pl.BlockSpec((3, 2 * d_ff), const2),
            pl.BlockSpec((1, 2 * d_ff), const2),
            pl.BlockSpec((d_ff, d), const2, pipeline_mode=pl.Buffered(1)),
            pl.BlockSpec((1, d), const2),
        ],
        out_specs=pl.BlockSpec((1, tm, d), lambda i, t: (i, t, 0)),
        out_shape=jax.ShapeDtypeStruct((b, seq, d), F32),
        scratch_shapes=[
            pltpu.VMEM((tm + 2 * halo, d), BF16),
            pltpu.VMEM((2, tm + 2 * halo, 2 * FF_CHUNK), F32),
            pltpu.VMEM((tm, d_ff), BF16),
        ],
        compiler_params=_cparams(("parallel", "arbitrary")),
        name="out_ffn",
    )(x, x, x, *mix_args, w_out, gate1, g, sc, sh, gate, wup, cw, cb, wdn, final_g)


def _layout_w_in(w):
    d = w.shape[0]
    ml_lo = N_MLA_IN
    hy_lo = N_MLA_IN + N_ML_IN
    gates = w[:, ml_lo + 3 * ML_WIDTH:hy_lo]
    pad = jnp.zeros((d, LANES - MLA_ROPE - 4 * ML_HEADS), w.dtype)
    return jnp.concatenate([w[:, :N_MLA_IN], gates, pad,
                            w[:, ml_lo:ml_lo + 3 * ML_WIDTH], w[:, hy_lo:]], axis=1).astype(BF16)


def _layout_w_uq(w):
    r = w.shape[0]
    w = w.reshape(r, MLA_HEADS, MLA_NOPE + MLA_ROPE)
    pad = jnp.zeros((r, MLA_HEADS, HEAD_PAD - MLA_NOPE - MLA_ROPE), w.dtype)
    return jnp.concatenate([w, pad], axis=-1).reshape(r, MLA_HEADS * HEAD_PAD).astype(BF16)


def _layout_w_ukv(w):
    r = w.shape[0]
    w = w.reshape(r, MLA_HEADS, MLA_NOPE + MLA_V)
    pad = jnp.zeros((r, MLA_HEADS, HEAD_PAD - MLA_NOPE), w.dtype)
    wk = jnp.concatenate([w[..., :MLA_NOPE], pad], axis=-1).reshape(r, MLA_HEADS * HEAD_PAD)
    wv = w[..., MLA_NOPE:].reshape(r, MLA_HEADS * MLA_V)
    return wk.astype(BF16), wv.astype(BF16)


def _block_diag(w):
    h, d, _ = w.shape
    eye = jnp.eye(h, dtype=w.dtype)
    return jnp.einsum('hde,hg->hdge', w, eye).reshape(h * d, h * d)


@functools.lru_cache(maxsize=None)
def _rope_tables(n_rows, rope):
    n_tok = n_rows * GRID_W
    cos_t = np.zeros((n_tok, HEAD_PAD), np.float32)
    sin_t = np.zeros((n_tok, HEAD_PAD), np.float32)
    cos_t[:, :ROPE_LANE + MLA_ROPE] = 1.0
    if rope:
        n_freq = MLA_ROPE // 4
        half = MLA_ROPE // 2
        inv = ROPE_THETA ** (-np.arange(n_freq, dtype=np.float64) / n_freq)
        row = np.repeat(np.arange(n_rows, dtype=np.float64), GRID_W)
        col = np.tile(np.arange(GRID_W, dtype=np.float64), n_rows)
        ang = np.concatenate([row[:, None] * inv, col[:, None] * inv], axis=-1)
        cos_t[:, ROPE_LANE:ROPE_LANE + half] = np.cos(ang)
        cos_t[:, ROPE_LANE + half:ROPE_LANE + MLA_ROPE] = np.cos(ang)
        sin_t[:, ROPE_LANE:ROPE_LANE + half] = -np.sin(ang)
        sin_t[:, ROPE_LANE + half:ROPE_LANE + MLA_ROPE] = np.sin(ang)
    return jnp.asarray(cos_t), jnp.asarray(sin_t)


def _ada_kernel(c_ref, w_ref, b_ref, o_ref):
    act = jax.nn.silu(c_ref[...])
    o_ref[0] = jnp.dot(act, w_ref[0], precision=HIGHEST, preferred_element_type=F32) + b_ref[0]


def _ada_mod(cvec, ada_w, ada_b):
    depth, d, n = ada_w.shape
    rows = cvec.shape[0]
    tn = n // 4
    return pl.pallas_call(
        _ada_kernel,
        grid=(depth, n // tn),
        in_specs=[
            pl.BlockSpec((rows, d), lambda l, j: (0, 0)),
            pl.BlockSpec((1, d, tn), lambda l, j: (l, 0, j)),
            pl.BlockSpec((1, 1, tn), lambda l, j: (l, 0, j)),
        ],
        out_specs=pl.BlockSpec((1, rows, tn), lambda l, j: (l, 0, j)),
        out_shape=jax.ShapeDtypeStruct((depth, rows, n), F32),
        compiler_params=_cparams(("parallel", "parallel")),
        name="ada_mod",
    )(cvec, ada_w, ada_b.reshape(depth, 1, n))


def kernel(x, c, ctx, c_ctx, ada_w, ada_b, norm1_g, norm2_g, w_in, mla_q_norm_g, mla_kv_norm_g,
           mla_w_uq, mla_w_ukv, ml_conv_w, ml_conv_b, ml_wq, ml_wk, ml_gate_b, ml_norm_g,
           hy_conv_w, hy_conv_b, hy_w1, hy_b1, hy_w2, hy_b2, hy_w3, hy_sin_freq, hy_bias_d,
           w_out, ffn_w_up, ffn_conv_w, ffn_conv_b, ffn_w_down, final_norm_g):
    b, seq, d = x.shape
    lc = ctx.shape[1]
    depth = ada_w.shape[0]
    tmx = min(512, seq)
    tmc = min(256, lc)
    cos_x, sin_x = _rope_tables(seq // GRID_W, True)
    cos_c, sin_c = _rope_tables(lc // GRID_W, False)
    head_mean = _block_diag(jnp.full((ML_HEADS, ML_HEAD_DIM, ML_HEAD_DIM), 1.0 / ML_HEAD_DIM, F32))
    row = lambda v: v.reshape(1, -1)

    n_cond = -(-(b + 1) // 8) * 8
    cvec = jnp.concatenate([c, c_ctx[None], jnp.zeros((n_cond - b - 1, d), c.dtype)], axis=0)
    mod = _ada_mod(cvec, ada_w, ada_b)

    for i in range(depth):
        last = i == depth - 1
        mx = [m[:, None, :] for m in jnp.split(mod[i, :b], 6, axis=-1)]
        mc = [m[:, None, :] for m in jnp.split(mod[i, b:b + 1], 6, axis=-1)]

        w_in_l = _layout_w_in(w_in[i])
        wq = _layout_w_uq(mla_w_uq[i])
        wk, wv = _layout_w_ukv(mla_w_ukv[i])
        wq_bd = _block_diag(ml_wq[i]).astype(BF16)
        wk_bd = (_block_diag(ml_wk[i]) * (ML_HEAD_DIM ** -0.5)).astype(BF16)
        gate_b = jnp.zeros((1, LANES), F32).at[0, GATE_LANE:GATE_LANE + 4 * ML_HEADS].set(
            ml_gate_b[i].reshape(-1))
        wup, wdn = ffn_w_up[i].astype(BF16), ffn_w_down[i].astype(BF16)
        fcw, fcb = ffn_conv_w[i], row(ffn_conv_b[i])
        w_out_l = w_out[i].astype(BF16)
        filt_w = (hy_w1[i], hy_b1[i], hy_w2[i], hy_b2[i], hy_w3[i], hy_sin_freq[i])
        g1 = row(norm1_g[i])

        ax_mla, ax_ml, ax_hy = _in_proj(x, g1, mx[1], mx[0], w_in_l, tmx)
        ac_mla, ac_ml, ac_hy = _in_proj(ctx, g1, mc[1], mc[0], w_in_l, tmc)

        gq, gkv = row(mla_q_norm_g[i]), row(mla_kv_norm_g[i])
        q_x, k_x, v_x = _mla_proj(ax_mla, gq, gkv, wq, wk, wv, cos_x, sin_x, tmx)
        q_c, k_c, v_c = _mla_proj(ac_mla, gq, gkv, wq, wk, wv, cos_c, sin_c, tmc)
        att_x = _attention(q_x, k_c, v_c, k_x, v_x, tmx, min(256, seq))

        mcw, mcb = ml_conv_w[i], row(ml_conv_b[i])
        feat_x = _ml_features(ax_ml, ax_mla, mcw, mcb, wq_bd, wk_bd, gate_b, tmx)
        feat_c = _ml_features(ac_ml, ac_mla, mcw, mcb, wq_bd, wk_bd, gate_b, tmc)
        ml_x, ml_c = _ml_scan(feat_c, ac_ml, feat_x, ax_ml, row(ml_norm_g[i]), head_mean)

        hcw, hcb, hbias = hy_conv_w[i], row(hy_conv_b[i]), row(hy_bias_d[i])
        hy_x = _hyena(ax_hy, hcw, hcb, filt_w, hbias, tmx, min(512, seq), 2)

        x = _out_ffn(x, att_x, ml_x, hy_x, w_out_l, mx[2], row(norm2_g[i]), mx[4], mx[3], mx[5],
                     wup, fcw, fcb, wdn, row(final_norm_g), tmx, last)

        if not last:
            att_c = _attention(q_c, k_c, v_c, None, None, tmc, tmc)
            hy_c = _hyena(ac_hy, hcw, hcb, filt_w, hbias, tmc, tmc, 2)
            ctx = _out_ffn(ctx, att_c, ml_c, hy_c, w_out_l, mc[2], row(norm2_g[i]), mc[4], mc[3],
                           mc[5], wup, fcw, fcb, wdn, row(final_norm_g), tmc, False)
    return x
```

```python
import functools
import math

import numpy as np
import jax
import jax.numpy as jnp
from jax import lax
from jax.experimental import pallas as pl
from jax.experimental.pallas import tpu as pltpu

F32 = jnp.float32
BF16 = jnp.bfloat16
HIGHEST = lax.Precision.HIGHEST

D_MODEL = 1024
DEPTH = 2
GRID_W = 64
EPS = 1e-6
MLA_HEADS = 8
MLA_NOPE = 64
MLA_ROPE = 32
MLA_V = 64
MLA_Q_RANK = 384
MLA_KV_RANK = 256
ROPE_THETA = 10000.0
ML_HEADS = 4
ML_HEAD_DIM = 64
ML_WIDTH = 256
HY_WIDTH = 256
HY_BANDS = 16
HY_DECAY_TARGET = 1e-2
HY_FAST_PCT = 0.3
HY_SLOW_PCT = 1.5
HY_SHIFT = 0.05
N_MLA_IN = MLA_Q_RANK + MLA_KV_RANK + MLA_ROPE
N_ML_IN = 3 * ML_WIDTH + 4 * ML_HEADS
D_FF = 2816

LANES = 128
HEAD_PAD = 128
SEG = 768
ROPE_LANE = 64
GATE_LANE = 32
ML_CHUNK = 128
FF_CHUNK = 256
VMEM_LIMIT = 56 * 1024 * 1024


def _cparams(sem):
    return pltpu.CompilerParams(dimension_semantics=sem, vmem_limit_bytes=VMEM_LIMIT)


def _lane_iota(shape):
    return lax.broadcasted_iota(jnp.int32, shape, len(shape) - 1)


def _dot_split3(x, rhs):
    out = None
    for _ in range(3):
        piece = x.astype(BF16)
        term = jnp.dot(piece, rhs, preferred_element_type=F32)
        out = term if out is None else out + term
        x = x - piece.astype(F32)
    return out


def _swap_rope_halves(x):
    width = x.shape[-1]
    lane = _lane_iota(x.shape) % HEAD_PAD
    lo = pltpu.roll(x, width - MLA_ROPE // 2, x.ndim - 1)
    hi = pltpu.roll(x, MLA_ROPE // 2, x.ndim - 1)
    return jnp.where(lane < ROPE_LANE + MLA_ROPE // 2, lo, hi)


def _attn_kernel(*refs, tk, n_kx):
    if n_kx:
        q_ref, kc_ref, vc_ref, kx_ref, vx_ref, o_ref = refs
    else:
        q_ref, kc_ref, vc_ref, o_ref = refs
    tq = q_ref.shape[1]
    nt_dims = (((1,), (1,)), ((), ()))
    qs = [q_ref[0, :, hh * HEAD_PAD:(hh + 1) * HEAD_PAD] for hh in range(2)]

    def step(q, k, v, carry):
        m, l, acc = carry
        s = lax.dot_general(q, k, nt_dims, preferred_element_type=F32)
        m_new = jnp.maximum(m, jnp.max(s, axis=-1, keepdims=True))
        p = jnp.exp2(s - m_new)
        alpha = jnp.exp2(m - m_new)
        part = p[:, :LANES]
        for j in range(1, p.shape[1] // LANES):
            part = part + p[:, j * LANES:(j + 1) * LANES]
        l = alpha * l + part
        acc = alpha * acc + jnp.dot(p.astype(BF16), v, preferred_element_type=F32)
        return m_new, l, acc

    def both_heads(k2, v2, carry):
        return tuple(step(qs[hh], k2[:, hh * HEAD_PAD:(hh + 1) * HEAD_PAD], v2, carry[hh])
                     for hh in range(2))

    init = (jnp.full((tq, 1), -jnp.inf, F32), jnp.zeros((tq, LANES), F32),
            jnp.zeros((tq, 2 * MLA_V), F32))
    carry = both_heads(kc_ref[0], vc_ref[0], (init, init))
    if n_kx:
        def body(i, c):
            rows = pl.ds(pl.multiple_of(i * tk, tk), tk)
            return both_heads(kx_ref[0, rows, :], vx_ref[0, rows, :], c)
        carry = lax.fori_loop(0, n_kx, body, carry, unroll=True)
    outs = [acc / jnp.sum(l, axis=-1, keepdims=True) for _, l, acc in carry]
    lane = _lane_iota(outs[0].shape)
    o_ref[0] = jnp.where(lane < MLA_V, outs[0], outs[1]).astype(o_ref.dtype)


def _attention(q, kc, vc, kx, vx, tq, tk):
    b, lq, _ = q.shape
    lc = kc.shape[1]
    n_pairs = MLA_HEADS // 2
    in_specs = [
        pl.BlockSpec((1, tq, 2 * HEAD_PAD), lambda i, p, t: (i, t, p)),
        pl.BlockSpec((1, lc, 2 * HEAD_PAD), lambda i, p, t: (i, 0, p)),
        pl.BlockSpec((1, lc, 2 * MLA_V), lambda i, p, t: (i, 0, p)),
    ]
    args = [q, kc, vc]
    n_kx = 0
    if kx is not None:
        lx = kx.shape[1]
        n_kx = lx // tk
        in_specs += [
            pl.BlockSpec((1, lx, 2 * HEAD_PAD), lambda i, p, t: (i, 0, p)),
            pl.BlockSpec((1, lx, 2 * MLA_V), lambda i, p, t: (i, 0, p)),
        ]
        args += [kx, vx]
    return pl.pallas_call(
        functools.partial(_attn_kernel, tk=tk, n_kx=n_kx),
        grid=(b, n_pairs, lq // tq),
        in_specs=in_specs,
        out_specs=pl.BlockSpec((1, tq, 2 * MLA_V), lambda i, p, t: (i, t, p)),
        out_shape=jax.ShapeDtypeStruct((b, lq, MLA_HEADS * MLA_V), BF16),
        compiler_params=_cparams(("parallel", "parallel", "arbitrary")),
        name="attention",
    )(*args)


def _halo_specs(seq, tm, halo, width):
    r = tm // halo
    last = seq // halo - 1
    prev = pl.BlockSpec((1, halo, width), lambda i, t: (i, jnp.maximum(t * r - 1, 0), 0))
    nxt = pl.BlockSpec((1, halo, width), lambda i, t: (i, jnp.minimum((t + 1) * r, last), 0))
    return prev, nxt


def _conv3(ext_ref, halo, tm, w, b):
    return (b + w[0:1] * ext_ref[pl.ds(halo - 1, tm), :]
            + w[1:2] * ext_ref[pl.ds(halo, tm), :]
            + w[2:3] * ext_ref[pl.ds(halo + 1, tm), :])


def _log_sigmoid(x):
    return jnp.minimum(x, 0.0) - jnp.log1p(jnp.exp(-jnp.abs(x)))


def _mixer_in_kernel(xp_ref, x_ref, xn_ref, g_ref, sc_ref, sh_ref, w_ref,
                     gq_ref, gkv_ref, wq_ref, wqs_ref, wk_ref, wv_ref, cos_ref, sin_ref,
                     mcw_ref, mcb_ref, mwq_ref, mwk_ref, gb_ref, hcw_ref, hcb_ref,
                     q_ref, k_ref, v_ref, mk_ref, mqt_ref, mvt_ref, pc_ref, pr_ref, og_ref,
                     x0_ref, z_ref, ext_scr, *, halo):
    tm = x_ref.shape[1]
    t = pl.program_id(1)
    x_ext = jnp.concatenate([xp_ref[0], x_ref[0], xn_ref[0]], axis=0)
    ms = jnp.mean(x_ext * x_ext, axis=-1, keepdims=True)
    h = x_ext * lax.rsqrt(ms + EPS) * g_ref[...] * (1.0 + sc_ref[0]) + sh_ref[0]
    row = lax.broadcasted_iota(jnp.int32, (tm + 2 * halo, 1), 0)
    outside = ((row < halo) & (t == 0)) | ((row >= tm + halo) & (t == pl.num_programs(1) - 1))
    hb = jnp.where(outside, 0.0, h).astype(BF16)
    for i in range(3):
        ext_scr[:, i * SEG:(i + 1) * SEG] = jnp.dot(hb, w_ref[:, i * SEG:(i + 1) * SEG],
                                                    preferred_element_type=F32)
    inner = pl.ds(halo, tm)

    cos = cos_ref[...]
    sin = sin_ref[...]
    scale = (MLA_NOPE + MLA_ROPE) ** -0.5 * math.log2(math.e)
    ql = ext_scr[inner, 0:MLA_Q_RANK]
    qn = ql * lax.rsqrt(jnp.mean(ql * ql, axis=-1, keepdims=True) + EPS) * gq_ref[...]
    qn = qn.astype(BF16)
    cos8 = jnp.concatenate([cos] * MLA_HEADS, axis=-1)
    sin8 = jnp.concatenate([sin] * MLA_HEADS, axis=-1)
    q = (jnp.dot(qn, wq_ref[...], preferred_element_type=F32) * cos8
         + jnp.dot(qn, wqs_ref[...], preferred_element_type=F32) * sin8) * scale
    q_ref[0] = q.astype(q_ref.dtype)
    kvl = ext_scr[inner, MLA_Q_RANK:MLA_Q_RANK + MLA_KV_RANK]
    kvn = kvl * lax.rsqrt(jnp.mean(kvl * kvl, axis=-1, keepdims=True) + EPS) * gkv_ref[...]
    kvb = kvn.astype(BF16)
    kn = jnp.dot(kvb, wk_ref[...], preferred_element_type=F32)
    v_ref[0] = jnp.dot(kvb, wv_ref[...], preferred_element_type=F32).astype(v_ref.dtype)
    blk = ext_scr[inner, SEG - LANES:SEG]
    kr = jnp.where(_lane_iota(blk.shape) < MLA_ROPE, blk, 0.0)
    kr = pltpu.roll(kr, ROPE_LANE, 1)
    kr = kr * cos + _swap_rope_halves(kr) * sin
    k_ref[0] = (kn + jnp.concatenate([kr] * MLA_HEADS, axis=-1)).astype(k_ref.dtype)

    n_chunk = tm // ML_CHUNK
    ml0 = SEG
    u = jax.nn.silu(_conv3(ext_scr.at[:, ml0:ml0 + ML_WIDTH], halo, tm, mcw_ref[...],
                           mcb_ref[...])).astype(BF16)
    mk_ref[0] = jnp.dot(u, mwk_ref[...], preferred_element_type=F32).astype(mk_ref.dtype)
    q_t = jnp.dot(u, mwq_ref[...], preferred_element_type=F32).T
    v_t = ext_scr[inner, ml0 + ML_WIDTH:ml0 + 2 * ML_WIDTH].T
    og_ref[0] = ext_scr[inner, ml0 + 2 * ML_WIDTH:ml0 + 3 * ML_WIDTH].astype(og_ref.dtype)
    xg = blk + gb_ref[...]
    lane = _lane_iota(xg.shape) - GATE_LANE
    is_gate = (lane >= 0) & (lane < 4 * ML_HEADS)
    is_forget = is_gate & ((lane // ML_HEADS) % 2 == 1)
    packed = jnp.where(is_forget, _log_sigmoid(xg), jnp.where(is_gate, xg, 0.0))
    lane_c = _lane_iota((ML_CHUNK, LANES)) - GATE_LANE
    fwd_lane = (lane_c >= ML_HEADS) & (lane_c < 2 * ML_HEADS)
    bwd_lane = (lane_c >= 3 * ML_HEADS) & (lane_c < 4 * ML_HEADS)
    r_idx = lax.broadcasted_iota(jnp.int32, (ML_CHUNK, ML_CHUNK), 0)
    c_idx = lax.broadcasted_iota(jnp.int32, (ML_CHUNK, ML_CHUNK), 1)
    lower = (c_idx <= r_idx).astype(BF16)
    for j in range(n_chunk):
        cols = slice(j * ML_CHUNK, (j + 1) * ML_CHUNK)
        part = packed[cols, :]
        prefix = None
        rest = part
        for _ in range(3):
            piece = rest.astype(BF16)
            term = jnp.dot(lower, piece, preferred_element_type=F32)
            prefix = term if prefix is None else prefix + term
            rest = rest - piece.astype(F32)
        suffix = prefix[ML_CHUNK - 1:ML_CHUNK, :] - prefix + part
        part = jnp.where(fwd_lane, prefix, jnp.where(bwd_lane, suffix, part))
        pc_ref[0, cols, :] = part
        pr_ref[0, j] = part.T
        mqt_ref[0, j] = q_t[:, cols].astype(mqt_ref.dtype)
        mvt_ref[0, j] = v_t[:, cols].astype(mvt_ref.dtype)

    hy0 = 2 * SEG
    uh = _conv3(ext_scr.at[:, hy0:hy0 + SEG], halo, tm, hcw_ref[...], hcb_ref[...])
    x0_ref[0] = uh[:, :HY_WIDTH].astype(x0_ref.dtype)
    z_ref[...] = (uh[:, HY_WIDTH:2 * HY_WIDTH] * uh[:, 2 * HY_WIDTH:]).astype(z_ref.dtype)


def _mixer_in(x, g, sc, sh, w_in, gq, gkv, wq, wqs, wk, wv, cos, sin, mcw, mcb, mwq, mwk, gate_b,
              hcw, hcb, tm):
    b, seq, d = x.shape
    halo = 8
    bm = sc.shape[0]
    mod_map = (lambda i, t: (i, 0, 0)) if bm > 1 else (lambda i, t: (0, 0, 0))
    prev, nxt = _halo_specs(seq, tm, halo, d)
    const = lambda i, t: (0, 0)
    whole = lambda a: pl.BlockSpec(a.shape, const, pipeline_mode=pl.Buffered(1))
    tok = lambda width: pl.BlockSpec((1, tm, width), lambda i, t: (i, t, 0))
    n_chunk = tm // ML_CHUNK
    chunked = lambda rows: pl.BlockSpec((1, n_chunk, rows, ML_CHUNK), lambda i, t: (i, t, 0, 0))
    hw = MLA_HEADS * HEAD_PAD
    vw = MLA_HEADS * MLA_V
    w = ML_WIDTH
    consts = (g, None, None, w_in, gq, gkv, wq, wqs, wk, wv)
    in_specs = [prev, tok(d), nxt]
    for a in consts:
        in_specs.append(pl.BlockSpec((1, 1, d), mod_map) if a is None else whole(a))
    in_specs += [pl.BlockSpec((tm, HEAD_PAD), lambda i, t: (t, 0))] * 2
    tail = (mcw, mcb, mwq, mwk, gate_b, hcw, hcb)
    in_specs += [whole(a) for a in tail]
    outs = pl.pallas_call(
        functools.partial(_mixer_in_kernel, halo=halo),
        grid=(b, seq // tm),
        in_specs=in_specs,
        out_specs=[
            tok(hw), tok(hw), tok(vw),
            tok(w), chunked(w), chunked(w), tok(LANES), chunked(LANES), tok(w),
            tok(HY_WIDTH),
            pl.BlockSpec((tm, HY_WIDTH), lambda i, t: (t, i)),
        ],
        out_shape=[
            jax.ShapeDtypeStruct((b, seq, hw), BF16),
            jax.ShapeDtypeStruct((b, seq, hw), BF16),
            jax.ShapeDtypeStruct((b, seq, vw), BF16),
            jax.ShapeDtypeStruct((b, seq, w), BF16),
            jax.ShapeDtypeStruct((b, seq // ML_CHUNK, w, ML_CHUNK), BF16),
            jax.ShapeDtypeStruct((b, seq // ML_CHUNK, w, ML_CHUNK), BF16),
            jax.ShapeDtypeStruct((b, seq, LANES), F32),
            jax.ShapeDtypeStruct((b, seq // ML_CHUNK, LANES, ML_CHUNK), F32),
            jax.ShapeDtypeStruct((b, seq, w), BF16),
            jax.ShapeDtypeStruct((b, seq, HY_WIDTH), BF16),
            jax.ShapeDtypeStruct((seq, b * HY_WIDTH), BF16),
        ],
        scratch_shapes=[pltpu.VMEM((tm + 2 * halo, 3 * SEG), F32)],
        compiler_params=_cparams(("parallel", "arbitrary")),
        name="mixer_in",
    )(x, x, x, g, sc, sh, w_in, gq, gkv, wq, wqs, wk, wv, cos, sin, *tail)
    return outs[0:3], outs[3:8], outs[8], outs[9:11]


def _ml_chunk(k, q_t, v_t, p_col, p_row, state_ref, m_ref, direction):
    t_len = k.shape[0]
    half = ML_HEAD_DIM
    s_idx = lax.broadcasted_iota(jnp.int32, (t_len, t_len), 0)
    t_idx = lax.broadcasted_iota(jnp.int32, (t_len, t_len), 1)
    keep = (s_idx <= t_idx) if direction == 0 else (s_idx >= t_idx)
    row_p = lax.broadcasted_iota(jnp.int32, (2 * half, t_len), 0)
    last = t_len - 1 if direction == 0 else 0
    outs = []
    for h in range(ML_HEADS):
        pair, odd = divmod(h, 2)
        sl = slice(pair * 2 * half, (pair + 1) * 2 * half)
        mine = (row_p >= half) if odd else (row_p < half)
        one_row = 0 if odd else half
        li_l = GATE_LANE + 2 * direction * ML_HEADS + h
        cu_l = li_l + ML_HEADS
        b_c = p_col[:, li_l:li_l + 1] - p_col[:, cu_l:cu_l + 1]
        li_r = p_row[li_l:li_l + 1, :]
        cu_r = p_row[cu_l:cu_l + 1, :]
        m = m_ref[h]
        state = state_ref[h]

        b_m = jnp.where(keep, b_c, -jnp.inf)
        m_out = cu_r + jnp.maximum(m, jnp.max(b_m, axis=0, keepdims=True))
        kh = k[:, sl]
        qh_t = jnp.where(mine, q_t[sl, :], 0).astype(BF16)
        vh_t = jnp.where(mine, v_t[sl, :].astype(F32), (row_p == one_row).astype(F32))
        s_t = jnp.dot(kh, qh_t, preferred_element_type=F32) * jnp.exp(b_m + (cu_r - m_out))
        a = jnp.exp(cu_r + m - m_out)
        lhs = jnp.concatenate([vh_t.astype(BF16), state.astype(BF16)], axis=1)
        rhs = jnp.concatenate([s_t.astype(BF16), (qh_t.astype(F32) * a).astype(BF16)], axis=0)
        tot = jnp.dot(lhs, rhs, preferred_element_type=F32)
        den = jnp.maximum(jnp.abs(tot[one_row:one_row + 1, :]), jnp.exp(-m_out))
        outs.append(tot * (1.0 / den))

        cum_last = cu_r[:, last:last + 1]
        g = cum_last - cu_r + li_r
        m_new = jnp.maximum(cum_last + m, jnp.max(g, axis=1, keepdims=True))
        wts = jnp.exp(g - m_new)
        decay = jnp.exp(cum_last + m - m_new)
        upd = jnp.dot((vh_t * wts).astype(BF16), kh, preferred_element_type=F32)
        state_ref[h] = decay * state + upd
        m_ref[h] = m_new
    pairs = []
    for pair in range(ML_HEADS // 2):
        pairs.append(jnp.where(row_p < half, outs[2 * pair], outs[2 * pair + 1]))
    return jnp.concatenate(pairs, axis=0)


def _ml_scan_kernel(kc_ref, qtc_ref, vtc_ref, pcc_ref, prc_ref, oc_ref,
                    kx_ref, qtx_ref, vtx_ref, pcx_ref, prx_ref, ox_ref, ng_ref, bd_ref,
                    hx_ref, hc_ref, fx_scr, bx_scr, fc_scr, bc_scr, state_scr, m_scr):
    t_len = ML_CHUNK
    n_c = kc_ref.shape[1] // t_len
    n_x = kx_ref.shape[1] // t_len
    state_scr[...] = jnp.zeros_like(state_scr)
    m_scr[...] = jnp.zeros_like(m_scr)

    def rows_of(i):
        return pl.ds(pl.multiple_of(i * t_len, t_len), t_len)

    def scan(refs, scrs, n):
        k_ref, qt_ref, vt_ref, pc_ref, pr_ref = refs

        def body(j, _):
            for direction, i in ((0, j), (1, n - 1 - j)):
                rows = rows_of(i)
                scrs[direction][i] = _ml_chunk(
                    k_ref[0, rows, :], qt_ref[0, i], vt_ref[0, i], pc_ref[0, rows, :],
                    pr_ref[0, i], state_scr.at[direction], m_scr.at[direction], direction)
            return 0

        lax.fori_loop(0, n, body, 0, unroll=2)

    def finish(scrs, o_ref, out_ref, n):
        def body(i, _):
            rows = rows_of(i)
            h = (scrs[0][i] + scrs[1][i]).T
            y = h * jax.nn.sigmoid(o_ref[0, rows, :].astype(F32))
            ms = _dot_split3(y * y, bd_ref[...])
            out_ref[0, rows, :] = (y * lax.rsqrt(ms + EPS) * ng_ref[...]).astype(out_ref.dtype)
            return 0

        lax.fori_loop(0, n, body, 0, unroll=2)

    scan((kc_ref, qtc_ref, vtc_ref, pcc_ref, prc_ref), (fc_scr, bc_scr), n_c)
    scan((kx_ref, qtx_ref, vtx_ref, pcx_ref, prx_ref), (fx_scr, bx_scr), n_x)
    finish((fc_scr, bc_scr), oc_ref, hc_ref, n_c)
    finish((fx_scr, bx_scr), ox_ref, hx_ref, n_x)


def _ml_scan(feat_c, gate_c, feat_x, gate_x, norm_g, bd):
    b, lc, w = feat_c[0].shape
    lx = feat_x[0].shape[1]

    def seq_specs(n):
        n_chunk = n // ML_CHUNK
        return [
            pl.BlockSpec((1, n, w), lambda i: (i, 0, 0)),
            pl.BlockSpec((1, n_chunk, w, ML_CHUNK), lambda i: (i, 0, 0, 0)),
            pl.BlockSpec((1, n_chunk, w, ML_CHUNK), lambda i: (i, 0, 0, 0)),
            pl.BlockSpec((1, n, LANES), lambda i: (i, 0, 0)),
            pl.BlockSpec((1, n_chunk, LANES, ML_CHUNK), lambda i: (i, 0, 0, 0)),
            pl.BlockSpec((1, n, w), lambda i: (i, 0, 0)),
        ]

    const = lambda i: (0, 0)
    return pl.pallas_call(
        _ml_scan_kernel,
        grid=(b,),
        in_specs=seq_specs(lc) + seq_specs(lx) + [
            pl.BlockSpec((1, w), const),
            pl.BlockSpec((w, w), const),
        ],
        out_specs=[
            pl.BlockSpec((1, lx, w), lambda i: (i, 0, 0)),
            pl.BlockSpec((1, lc, w), lambda i: (i, 0, 0)),
        ],
        out_shape=[
            jax.ShapeDtypeStruct((b, lx, w), BF16),
            jax.ShapeDtypeStruct((b, lc, w), BF16),
        ],
        scratch_shapes=[
            pltpu.VMEM((lx // ML_CHUNK, w, ML_CHUNK), F32),
            pltpu.VMEM((lx // ML_CHUNK, w, ML_CHUNK), F32),
            pltpu.VMEM((lc // ML_CHUNK, w, ML_CHUNK), F32),
            pltpu.VMEM((lc // ML_CHUNK, w, ML_CHUNK), F32),
            pltpu.VMEM((2, ML_HEADS, 2 * ML_HEAD_DIM, 2 * ML_HEAD_DIM), F32),
            pltpu.VMEM((2, ML_HEADS, 1, 1), F32),
        ],
        compiler_params=_cparams(("parallel",)),
        name="ml_scan",
    )(*feat_c, gate_c, *feat_x, gate_x, norm_g, bd)


@functools.lru_cache(maxsize=None)
def _dft_mats(seq, kt):
    n_fft = 2 * seq
    k = np.arange(seq, dtype=np.int64)[:, None]
    n = np.arange(seq, dtype=np.int64)[None, :]
    ang = (2.0 * np.pi / n_fft) * ((k * n) % n_fft).astype(np.float64)
    cos = np.cos(ang)
    msin = -np.sin(ang)
    msin[0, :] = 1.0 - 2.0 * (np.arange(seq) % 2)
    fwd = np.stack([cos.reshape(seq // kt, kt, seq), msin.reshape(seq // kt, kt, seq)], axis=1)
    fwd = fwd.reshape(2 * seq, seq)
    wgt = np.full((seq, 1), 2.0 / n_fft)
    wgt[0, 0] = 1.0 / n_fft
    inv = np.stack([(cos * wgt).reshape(seq // kt, kt, seq),
                    (msin * wgt).reshape(seq // kt, kt, seq)], axis=1)
    inv = inv.reshape(2 * seq, seq).T
    return jnp.asarray(fwd, dtype=BF16), jnp.asarray(np.ascontiguousarray(inv), dtype=BF16)


def _dft_fwd_kernel(f_ref, z_ref, *rest, kt, reps):
    zz = jnp.dot(f_ref[...], z_ref[...], preferred_element_type=F32)
    if not rest[1:]:
        rest[0][...] = zz
        return
    h_ref, l1_ref, y_ref = rest
    w = HY_WIDTH
    inv_l1 = 1.0 / l1_ref[...]
    hc, hs = h_ref[:kt, :], h_ref[kt:, :]
    first = (lax.broadcasted_iota(jnp.int32, (kt, w), 0) == 0) & (pl.program_id(0) == 0)
    hre = (hc[:, :w] + hc[:, w:]) * inv_l1
    him = jnp.where(first, 0.0, (hs[:, :w] - hs[:, w:]) * inv_l1)
    hre2 = jnp.where(first, (hs[:, :w] + hs[:, w:]) * inv_l1, hre)
    tile = lambda r: jnp.concatenate([r] * reps, axis=-1)
    hre, him, hre2 = tile(hre), tile(him), tile(hre2)
    zre, zim = zz[:kt], zz[kt:]
    y_ref[:kt, :] = (zre * hre - zim * him).astype(y_ref.dtype)
    y_ref[kt:, :] = (zre * him + zim * hre2).astype(y_ref.dtype)


def _dft_fwd(fwd, z, spectrum, kt, cb):
    seq = z.shape[0]
    ncol = z.shape[1]
    in_specs = [
        pl.BlockSpec((2 * kt, seq), lambda i, j: (i, 0)),
        pl.BlockSpec((seq, cb), lambda i, j: (0, j)),
    ]
    args = [fwd, z]
    out_dtype = F32
    if spectrum is not None:
        in_specs += [pl.BlockSpec((2 * kt, 2 * HY_WIDTH), lambda i, j: (i, 0)),
                     pl.BlockSpec((1, HY_WIDTH), lambda i, j: (0, 0))]
        args += list(spectrum)
        out_dtype = BF16
    return pl.pallas_call(
        functools.partial(_dft_fwd_kernel, kt=kt, reps=cb // HY_WIDTH),
        grid=(seq // kt, ncol // cb),
        in_specs=in_specs,
        out_specs=pl.BlockSpec((2 * kt, cb), lambda i, j: (i, j)),
        out_shape=jax.ShapeDtypeStruct((2 * seq, ncol), out_dtype),
        compiler_params=_cparams(("parallel", "arbitrary")),
        name="dft_fwd",
    )(*args)


def _dft_inv_kernel(g_ref, y_ref, z_ref, x0_ref, bias_ref, o_ref, *, nb):
    y = jnp.dot(g_ref[...], y_ref[...], preferred_element_type=F32)
    bias = bias_ref[...]
    for bb in range(nb):
        cols = slice(bb * HY_WIDTH, (bb + 1) * HY_WIDTH)
        yb = y[:, cols] + z_ref[:, cols].astype(F32) * bias
        o_ref[bb] = (x0_ref[bb].astype(F32) * yb).astype(o_ref.dtype)


def _dft_inv(inv, y, z, x0, bias, tm, nb):
    b, seq, _ = x0.shape
    cb = nb * HY_WIDTH
    return pl.pallas_call(
        functools.partial(_dft_inv_kernel, nb=nb),
        grid=(seq // tm, b // nb),
        in_specs=[
            pl.BlockSpec((tm, 2 * seq), lambda t, j: (t, 0)),
            pl.BlockSpec((2 * seq, cb), lambda t, j: (0, j)),
            pl.BlockSpec((tm, cb), lambda t, j: (t, j)),
            pl.BlockSpec((nb, tm, HY_WIDTH), lambda t, j: (j, t, 0)),
            pl.BlockSpec((1, HY_WIDTH), lambda t, j: (0, 0)),
        ],
        out_specs=pl.BlockSpec((nb, tm, HY_WIDTH), lambda t, j: (j, t, 0)),
        out_shape=jax.ShapeDtypeStruct((b, seq, HY_WIDTH), BF16),
        compiler_params=_cparams(("parallel", "arbitrary")),
        name="dft_inv",
    )(inv, y, z, x0, bias)


@functools.lru_cache(maxsize=None)
def _filter_features(seq):
    t = np.linspace(0.0, 1.0, seq)[:, None]
    omega = 2.0 * math.pi * np.arange(seq) / seq
    bands = np.linspace(1e-4, HY_BANDS - 1, HY_BANDS)
    ang = omega[:, None] * bands[None, :]
    feats = np.zeros((seq, LANES), np.float32)
    feats[:, :1 + 2 * HY_BANDS] = np.concatenate([t, np.cos(ang), -np.sin(ang)], axis=-1)
    deltas = np.abs(np.linspace(math.log(HY_DECAY_TARGET) / HY_SLOW_PCT,
                                math.log(HY_DECAY_TARGET) / HY_FAST_PCT, HY_WIDTH))
    return jnp.asarray(feats), jnp.asarray(deltas[None, :].astype(np.float32))


def _filter_kernel(z_ref, dl_ref, w1_ref, b1_ref, w2_ref, b2_ref, w3_ref, fr_ref, taps_ref,
                   l1_ref):
    t = pl.program_id(0)
    z = z_ref[...]
    fr = fr_ref[...]
    dot = functools.partial(jnp.dot, precision=HIGHEST, preferred_element_type=F32)
    hdn = jnp.sin(fr * (dot(z, w1_ref[...]) + b1_ref[...]))
    hdn = jnp.sin(fr * (dot(hdn, w2_ref[...]) + b2_ref[...]))
    filt = dot(hdn, w3_ref[...])
    window = jnp.exp(-z[:, 0:1] * dl_ref[...]) + HY_SHIFT
    h_fwd = filt[:, :HY_WIDTH] * window
    h_bwd = filt[:, HY_WIDTH:] * window
    first = (lax.broadcasted_iota(jnp.int32, h_bwd.shape, 0) == 0) & (t == 0)
    h_bwd = jnp.where(first, 0.0, h_bwd)
    taps_ref[...] = jnp.concatenate([h_fwd, h_bwd], axis=-1).astype(taps_ref.dtype)
    part = jnp.sum(jnp.abs(h_fwd) + jnp.abs(h_bwd), axis=0, keepdims=True)

    @pl.when(t == 0)
    def _():
        l1_ref[...] = part

    @pl.when(t > 0)
    def _():
        l1_ref[...] += part


def _hyena_filter(seq, tm, w1, b1, w2, b2, w3, sin_freq):
    feats, deltas = _filter_features(seq)
    hid = w2.shape[0]
    w1p = jnp.zeros((LANES, hid), F32).at[:w1.shape[0]].set(w1)
    const = lambda t: (0, 0)
    row = lambda v: v.reshape(1, -1)
    return pl.pallas_call(
        _filter_kernel,
        grid=(seq // tm,),
        in_specs=[
            pl.BlockSpec((tm, LANES), lambda t: (t, 0)),
            pl.BlockSpec((1, HY_WIDTH), const),
            pl.BlockSpec((LANES, hid), const),
            pl.BlockSpec((1, hid), const),
            pl.BlockSpec((hid, hid), const),
            pl.BlockSpec((1, hid), const),
            pl.BlockSpec((hid, 2 * HY_WIDTH), const),
            pl.BlockSpec((1, hid), const),
        ],
        out_specs=[
            pl.BlockSpec((tm, 2 * HY_WIDTH), lambda t: (t, 0)),
            pl.BlockSpec((1, HY_WIDTH), const),
        ],
        out_shape=[
            jax.ShapeDtypeStruct((seq, 2 * HY_WIDTH), BF16),
            jax.ShapeDtypeStruct((1, HY_WIDTH), F32),
        ],
        compiler_params=_cparams(("arbitrary",)),
        name="hy_filter",
    )(feats, deltas, w1p, row(b1), w2, row(b2), w3, row(sin_freq))


FFT_RADIX = 16


@functools.lru_cache(maxsize=None)
def _fft_consts(seq):
    n_fft = 2 * seq
    s_half = FFT_RADIX // 2
    f_len = n_fft // FFT_RADIX
    ks = np.arange(s_half + 1)
    theta = 2.0 * np.pi * np.outer(ks, np.arange(s_half)) / FFT_RADIX
    wgt = np.where((ks == 0) | (ks == s_half), 1.0, 2.0)[:, None] / n_fft
    coef = np.concatenate([np.cos(theta), -np.sin(theta), wgt * np.cos(theta),
                           -wgt * np.sin(theta)], axis=1).astype(np.float32)
    f = np.arange(f_len, dtype=np.int64)
    fwd, inv = [], []
    for k0 in ks:
        k = k0 + FFT_RADIX * np.arange(f_len, dtype=np.int64)
        ang = (2.0 * np.pi / n_fft) * ((k[:, None] * f[None, :]) % n_fft)
        m_re, m_im = np.cos(ang), -np.sin(ang)
        fwd.append(np.concatenate([m_re, m_im], axis=0))
        inv.append(np.concatenate([m_re.T, m_im.T], axis=0))
    return (jnp.asarray(coef), jnp.asarray(np.stack(fwd), dtype=BF16),
            jnp.asarray(np.stack(inv), dtype=BF16))


def _fft_forward(coef_ref, ks, z_ref, fm_ref, f_len, width):
    s_half = FFT_RADIX // 2
    a_re = a_im = None
    for s in range(s_half):
        blk = z_ref[s * f_len:(s + 1) * f_len, :].astype(F32)
        t_re, t_im = coef_ref[ks, s] * blk, coef_ref[ks, s_half + s] * blk
        a_re = t_re if a_re is None else a_re + t_re
        a_im = t_im if a_im is None else a_im + t_im
    p = jnp.dot(fm_ref[0], jnp.concatenate([a_re, a_im], axis=1).astype(BF16),
                preferred_element_type=F32)
    x_re = p[:f_len, :width] - p[f_len:, width:]
    x_im = p[f_len:, :width] + p[:f_len, width:]
    return x_re, x_im


def _fft_filter_kernel(coef_ref, taps_ref, fm_ref, o_ref, *, f_len):
    x_re, x_im = _fft_forward(coef_ref, pl.program_id(0), taps_ref, fm_ref, f_len,
                              taps_ref.shape[1])
    o_ref[0, :f_len, :] = x_re
    o_ref[0, f_len:, :] = x_im


def _fft_conv_kernel(coef_ref, z_ref, fm_ref, im_ref, h_ref, l1_ref, x0_ref, bias_ref, o_ref,
                     y_scr, *, f_len):
    ks = pl.program_id(1)
    s_half = FFT_RADIX // 2
    w = HY_WIDTH
    x_re, x_im = _fft_forward(coef_ref, ks, z_ref, fm_ref, f_len, w)
    inv_l1 = 1.0 / l1_ref[...]
    h_re = (h_ref[0, :f_len, :w] + h_ref[0, :f_len, w:]) * inv_l1
    h_im = (h_ref[0, f_len:, :w] - h_ref[0, f_len:, w:]) * inv_l1
    y_re = x_re * h_re - x_im * h_im
    y_im = x_re * h_im + x_im * h_re
    r = jnp.dot(im_ref[0], jnp.concatenate([y_re, y_im], axis=1).astype(BF16),
                preferred_element_type=F32)
    b_re = r[:f_len, :w] + r[f_len:, w:]
    b_im = r[:f_len, w:] - r[f_len:, :w]

    @pl.when(ks == 0)
    def _():
        y_scr[...] = jnp.zeros_like(y_scr)

    for s in range(s_half):
        rows = slice(s * f_len, (s + 1) * f_len)
        y_scr[rows, :] += (coef_ref[ks, 2 * s_half + s] * b_re
                           + coef_ref[ks, 3 * s_half + s] * b_im)

    @pl.when(ks == s_half)
    def _():
        y = y_scr[...] + z_ref[...].astype(F32) * bias_ref[...]
        o_ref[0] = (x0_ref[0].astype(F32) * y).astype(o_ref.dtype)


def _hyena_fft(z, x0, taps, l1, bias_d):
    b, seq, w = x0.shape
    coef, fwd_m, inv_m = _fft_consts(seq)
    n_ks = FFT_RADIX // 2 + 1
    f_len = 2 * seq // FFT_RADIX
    smem = pl.BlockSpec(memory_space=pltpu.SMEM)
    h_raw = pl.pallas_call(
        functools.partial(_fft_filter_kernel, f_len=f_len),
        grid=(n_ks,),
        in_specs=[smem,
                  pl.BlockSpec((seq, 2 * w), lambda k: (0, 0)),
                  pl.BlockSpec((1, 2 * f_len, f_len), lambda k: (k, 0, 0))],
        out_specs=pl.BlockSpec((1, 2 * f_len, 2 * w), lambda k: (k, 0, 0)),
        out_shape=jax.ShapeDtypeStruct((n_ks, 2 * f_len, 2 * w), F32),
        compiler_params=_cparams(("parallel",)),
        name="hy_fft_filter",
    )(coef, taps, fwd_m)
    return pl.pallas_call(
        functools.partial(_fft_conv_kernel, f_len=f_len),
        grid=(b, n_ks),
        in_specs=[smem,
                  pl.BlockSpec((seq, w), lambda i, k: (0, i)),
                  pl.BlockSpec((1, 2 * f_len, f_len), lambda i, k: (k, 0, 0)),
                  pl.BlockSpec((1, 2 * f_len, f_len), lambda i, k: (k, 0, 0)),
                  pl.BlockSpec((1, 2 * f_len, 2 * w), lambda i, k: (k, 0, 0)),
                  pl.BlockSpec((1, w), lambda i, k: (0, 0)),
                  pl.BlockSpec((1, seq, w), lambda i, k: (i, 0, 0)),
                  pl.BlockSpec((1, w), lambda i, k: (0, 0))],
        out_specs=pl.BlockSpec((1, seq, w), lambda i, k: (i, 0, 0)),
        out_shape=jax.ShapeDtypeStruct((b, seq, w), BF16),
        scratch_shapes=[pltpu.VMEM((seq, w), F32)],
        compiler_params=_cparams(("parallel", "arbitrary")),
        name="hy_fft_conv",
    )(coef, z, fwd_m, inv_m, h_raw, l1, x0, bias_d)


def _hyena(x0, z, filt_w, bias_d, tm, kt, nb):
    seq = x0.shape[1]
    taps, l1 = _hyena_filter(seq, tm, *filt_w)
    if 2 * seq // FFT_RADIX >= LANES:
        return _hyena_fft(z, x0, taps, l1, bias_d)
    fwd, inv = _dft_mats(seq, kt)
    raw = _dft_fwd(fwd, taps, None, kt, 2 * HY_WIDTH)
    y = _dft_fwd(fwd, z, (raw, l1), kt, nb * HY_WIDTH)
    return _dft_inv(inv, y, z, x0, bias_d, tm, nb)


def _ffn_kernel(xp_ref, x_ref, xn_ref, ap_ref, a_ref, an_ref, mp_ref, m_ref, mn_ref, hp_ref,
                h_ref, hn_ref, wo_ref, g1_ref, g_ref, sc_ref, sh_ref, gate_ref, wup_ref, cw_ref,
                cb_ref, wdn_ref, fg_ref, o_ref, xe_scr, u_scr, acc_scr, *, halo, mix_halo,
                final_norm):
    tm = x_ref.shape[1]
    d_ff = wdn_ref.shape[0]
    n_chunks = d_ff // FF_CHUNK
    t = pl.program_id(1)

    def with_halo(p_ref, c_ref, n_ref):
        prev = p_ref[0].astype(F32)[mix_halo - halo:, :]
        nxt = n_ref[0].astype(F32)[:halo, :]
        return jnp.concatenate([prev, c_ref[0].astype(F32), nxt], axis=0).astype(BF16)

    lo = 0
    mixed = None
    for refs in ((ap_ref, a_ref, an_ref), (mp_ref, m_ref, mn_ref), (hp_ref, h_ref, hn_ref)):
        width = refs[1].shape[2]
        part = jnp.dot(with_halo(*refs), wo_ref[lo:lo + width, :], preferred_element_type=F32)
        mixed = part if mixed is None else mixed + part
        lo += width
    x_ext = jnp.concatenate([xp_ref[0], x_ref[0], xn_ref[0]], axis=0) + g1_ref[0] * mixed

    def cols(c):
        return (slice(c * FF_CHUNK, (c + 1) * FF_CHUNK),
                slice(d_ff + c * FF_CHUNK, d_ff + (c + 1) * FF_CHUNK))

    def norm_mod(x):
        ms = jnp.mean(x * x, axis=-1, keepdims=True)
        return x * lax.rsqrt(ms + EPS) * g_ref[...] * (1.0 + sc_ref[0]) + sh_ref[0]

    row = lax.broadcasted_iota(jnp.int32, (tm + 2 * halo, 1), 0)
    outside = ((row < halo) & (t == 0)) | ((row >= tm + halo) & (t == pl.num_programs(1) - 1))
    xe_scr[...] = jnp.where(outside, 0.0, norm_mod(x_ext)).astype(BF16)
    x_mid = x_ext[halo:halo + tm, :]

    def up_proj(c, slot):
        xe_b = xe_scr[...]
        for half, sl in enumerate(cols(c)):
            u_scr[slot, :, half * FF_CHUNK:(half + 1) * FF_CHUNK] = jnp.dot(
                xe_b, wup_ref[:, sl], preferred_element_type=F32)

    up_proj(0, 0)
    for c in range(n_chunks):
        slot = c % 2
        if c + 1 < n_chunks:
            up_proj(c + 1, 1 - slot)
        gate_cols, val_cols = cols(c)
        cw = jnp.concatenate([cw_ref[:, gate_cols], cw_ref[:, val_cols]], axis=-1)
        cb = jnp.concatenate([cb_ref[:, gate_cols], cb_ref[:, val_cols]], axis=-1)
        uc = _conv3(u_scr.at[slot], halo, tm, cw, cb)
        act = jax.nn.silu(uc[:, :FF_CHUNK]) * uc[:, FF_CHUNK:]
        acc_scr[:, gate_cols] = act.astype(BF16)
    down = jnp.dot(acc_scr[...], wdn_ref[...], preferred_element_type=F32)
    y = x_mid + gate_ref[0] * down
    if final_norm:
        y = y * lax.rsqrt(jnp.mean(y * y, axis=-1, keepdims=True) + EPS) * fg_ref[...]
    o_ref[0] = y


def _out_ffn(x, att, ml, hy, w_out, gate1, g, sc, sh, gate, wup, cw, cb, wdn, final_g, tm,
             final_norm):
    b, seq, d = x.shape
    halo = 8
    mix_halo = 16
    bm = sc.shape[0]
    mod_map = (lambda i, t: (i, 0, 0)) if bm > 1 else (lambda i, t: (0, 0, 0))
    prev, nxt = _halo_specs(seq, tm, halo, d)
    const2 = lambda i, t: (0, 0)
    d_ff = wdn.shape[0]
    mix_specs, mix_args = [], []
    for a in (att, ml, hy):
        width = a.shape[2]
        m_prev, m_nxt = _halo_specs(seq, tm, mix_halo, width)
        mix_specs += [m_prev, pl.BlockSpec((1, tm, width), lambda i, t: (i, t, 0)), m_nxt]
        mix_args += [a, a, a]
    return pl.pallas_call(
        functools.partial(_ffn_kernel, halo=halo, mix_halo=mix_halo, final_norm=final_norm),
        grid=(b, seq // tm),
        in_specs=[
            prev,
            pl.BlockSpec((1, tm, d), lambda i, t: (i, t, 0)),
            nxt,
            *mix_specs,
            pl.BlockSpec((d, d), const2, pipeline_mode=pl.Buffered(1)),
            pl.BlockSpec((1, 1, d), mod_map),
            pl.BlockSpec((1, d), const2),
            pl.BlockSpec((1, 1, d), mod_map),
            pl.BlockSpec((1, 1, d), mod_map),
            pl.BlockSpec((1, 1, d), mod_map),
            pl.BlockSpec((d, 2 * d_ff), const2, pipeline_mode=pl.Buffered(1)),
            pl.BlockSpec((3, 2 * d_ff), const2),
            pl.BlockSpec((1, 2 * d_ff), const2),
            pl.BlockSpec((d_ff, d), const2, pipeline_mode=pl.Buffered(1)),
            pl.BlockSpec((1, d), const2),
        ],
        out_specs=pl.BlockSpec((1, tm, d), lambda i, t: (i, t, 0)),
        out_shape=jax.ShapeDtypeStruct((b, seq, d), F32),
        scratch_shapes=[
            pltpu.VMEM((tm + 2 * halo, d), BF16),
            pltpu.VMEM((2, tm + 2 * halo, 2 * FF_CHUNK), F32),
            pltpu.VMEM((tm, d_ff), BF16),
        ],
        compiler_params=_cparams(("parallel", "arbitrary")),
        name="out_ffn",
    )(x, x, x, *mix_args, w_out, gate1, g, sc, sh, gate, wup, cw, cb, wdn, final_g)


def _layout_w_in(w):
    d = w.shape[0]
    ml_lo = N_MLA_IN
    hy_lo = N_MLA_IN + N_ML_IN
    gates = w[:, ml_lo + 3 * ML_WIDTH:hy_lo]
    pad = jnp.zeros((d, LANES - MLA_ROPE - 4 * ML_HEADS), w.dtype)
    return jnp.concatenate([w[:, :N_MLA_IN], gates, pad,
                            w[:, ml_lo:ml_lo + 3 * ML_WIDTH], w[:, hy_lo:]], axis=1).astype(BF16)


def _layout_w_uq(w):
    r = w.shape[0]
    w = w.reshape(r, MLA_HEADS, MLA_NOPE + MLA_ROPE)
    pad = jnp.zeros((r, MLA_HEADS, HEAD_PAD - MLA_NOPE - MLA_ROPE), w.dtype)
    return jnp.concatenate([w, pad], axis=-1).reshape(r, MLA_HEADS * HEAD_PAD).astype(BF16)


def _layout_w_uq_swapped(w):
    r = w.shape[0]
    w = w.reshape(r, MLA_HEADS, MLA_NOPE + MLA_ROPE)
    half = MLA_ROPE // 2
    lo = jnp.zeros((r, MLA_HEADS, ROPE_LANE), w.dtype)
    hi = jnp.zeros((r, MLA_HEADS, HEAD_PAD - ROPE_LANE - MLA_ROPE), w.dtype)
    out = jnp.concatenate([lo, w[..., MLA_NOPE + half:], w[..., MLA_NOPE:MLA_NOPE + half], hi],
                          axis=-1)
    return out.reshape(r, MLA_HEADS * HEAD_PAD).astype(BF16)


def _layout_w_ukv(w):
    r = w.shape[0]
    w = w.reshape(r, MLA_HEADS, MLA_NOPE + MLA_V)
    pad = jnp.zeros((r, MLA_HEADS, HEAD_PAD - MLA_NOPE), w.dtype)
    wk = jnp.concatenate([w[..., :MLA_NOPE], pad], axis=-1).reshape(r, MLA_HEADS * HEAD_PAD)
    wv = w[..., MLA_NOPE:].reshape(r, MLA_HEADS * MLA_V)
    return wk.astype(BF16), wv.astype(BF16)


def _block_diag(w):
    h, d, _ = w.shape
    eye = jnp.eye(h, dtype=w.dtype)
    return jnp.einsum('hde,hg->hdge', w, eye).reshape(h * d, h * d)


@functools.lru_cache(maxsize=None)
def _rope_tables(n_rows, rope):
    n_tok = n_rows * GRID_W
    cos_t = np.zeros((n_tok, HEAD_PAD), np.float32)
    sin_t = np.zeros((n_tok, HEAD_PAD), np.float32)
    cos_t[:, :ROPE_LANE + MLA_ROPE] = 1.0
    if rope:
        n_freq = MLA_ROPE // 4
        half = MLA_ROPE // 2
        inv = ROPE_THETA ** (-np.arange(n_freq, dtype=np.float64) / n_freq)
        row = np.repeat(np.arange(n_rows, dtype=np.float64), GRID_W)
        col = np.tile(np.arange(GRID_W, dtype=np.float64), n_rows)
        ang = np.concatenate([row[:, None] * inv, col[:, None] * inv], axis=-1)
        cos_t[:, ROPE_LANE:ROPE_LANE + half] = np.cos(ang)
        cos_t[:, ROPE_LANE + half:ROPE_LANE + MLA_ROPE] = np.cos(ang)
        sin_t[:, ROPE_LANE:ROPE_LANE + half] = -np.sin(ang)
        sin_t[:, ROPE_LANE + half:ROPE_LANE + MLA_ROPE] = np.sin(ang)
    return jnp.asarray(cos_t), jnp.asarray(sin_t)


def _ada_kernel(c_ref, w_ref, b_ref, o_ref):
    act = jax.nn.silu(c_ref[...])
    o_ref[0] = jnp.dot(act, w_ref[0], precision=HIGHEST, preferred_element_type=F32) + b_ref[0]


def _ada_mod(cvec, ada_w, ada_b):
    depth, d, n = ada_w.shape
    rows = cvec.shape[0]
    tn = n // 4
    return pl.pallas_call(
        _ada_kernel,
        grid=(depth, n // tn),
        in_specs=[
            pl.BlockSpec((rows, d), lambda l, j: (0, 0)),
            pl.BlockSpec((1, d, tn), lambda l, j: (l, 0, j)),
            pl.BlockSpec((1, 1, tn), lambda l, j: (l, 0, j)),
        ],
        out_specs=pl.BlockSpec((1, rows, tn), lambda l, j: (l, 0, j)),
        out_shape=jax.ShapeDtypeStruct((depth, rows, n), F32),
        compiler_params=_cparams(("parallel", "parallel")),
        name="ada_mod",
    )(cvec, ada_w, ada_b.reshape(depth, 1, n))


def kernel(x, c, ctx, c_ctx, ada_w, ada_b, norm1_g, norm2_g, w_in, mla_q_norm_g, mla_kv_norm_g,
           mla_w_uq, mla_w_ukv, ml_conv_w, ml_conv_b, ml_wq, ml_wk, ml_gate_b, ml_norm_g,
           hy_conv_w, hy_conv_b, hy_w1, hy_b1, hy_w2, hy_b2, hy_w3, hy_sin_freq, hy_bias_d,
           w_out, ffn_w_up, ffn_conv_w, ffn_conv_b, ffn_w_down, final_norm_g):
    b, seq, d = x.shape
    lc = ctx.shape[1]
    depth = ada_w.shape[0]
    tmx = min(512, seq)
    tmc = min(256, lc)
    cos_x, sin_x = _rope_tables(seq // GRID_W, True)
    cos_c, sin_c = _rope_tables(lc // GRID_W, False)
    head_mean = _block_diag(
        jnp.full((ML_HEADS, ML_HEAD_DIM, ML_HEAD_DIM), 1.0 / ML_HEAD_DIM, F32)).astype(BF16)
    row = lambda v: v.reshape(1, -1)

    n_cond = -(-(b + 1) // 8) * 8
    cvec = jnp.concatenate([c, c_ctx[None], jnp.zeros((n_cond - b - 1, d), c.dtype)], axis=0)
    mod = _ada_mod(cvec, ada_w, ada_b)

    for i in range(depth):
        last = i == depth - 1
        mx = [m[:, None, :] for m in jnp.split(mod[i, :b], 6, axis=-1)]
        mc = [m[:, None, :] for m in jnp.split(mod[i, b:b + 1], 6, axis=-1)]

        w_in_l = _layout_w_in(w_in[i])
        wq = _layout_w_uq(mla_w_uq[i])
        wk, wv = _layout_w_ukv(mla_w_ukv[i])
        wq_bd = _block_diag(ml_wq[i]).astype(BF16)
        wk_bd = (_block_diag(ml_wk[i]) * (ML_HEAD_DIM ** -0.5)).astype(BF16)
        gate_b = jnp.zeros((1, LANES), F32).at[0, GATE_LANE:GATE_LANE + 4 * ML_HEADS].set(
            ml_gate_b[i].reshape(-1))
        wup, wdn = ffn_w_up[i].astype(BF16), ffn_w_down[i].astype(BF16)
        fcw, fcb = ffn_conv_w[i], row(ffn_conv_b[i])
        w_out_l = w_out[i].astype(BF16)
        filt_w = (hy_w1[i], hy_b1[i], hy_w2[i], hy_b2[i], hy_w3[i], hy_sin_freq[i])
        g1 = row(norm1_g[i])

        gq, gkv = row(mla_q_norm_g[i]), row(mla_kv_norm_g[i])
        wqs = _layout_w_uq_swapped(mla_w_uq[i])
        mcw, mcb = ml_conv_w[i], row(ml_conv_b[i])
        hcw, hcb, hbias = hy_conv_w[i], row(hy_conv_b[i]), row(hy_bias_d[i])
        shared = (mcw, mcb, wq_bd, wk_bd, gate_b, hcw, hcb)
        (q_x, k_x, v_x), feat_x, og_x, (x0_x, z_x) = _mixer_in(
            x, g1, mx[1], mx[0], w_in_l, gq, gkv, wq, wqs, wk, wv, cos_x, sin_x, *shared, tmx)
        (q_c, k_c, v_c), feat_c, og_c, (x0_c, z_c) = _mixer_in(
            ctx, g1, mc[1], mc[0], w_in_l, gq, gkv, wq, wqs, wk, wv, cos_c, sin_c, *shared, tmc)

        att_x = _attention(q_x, k_c, v_c, k_x, v_x, tmx, min(256, seq))
        ml_x, ml_c = _ml_scan(feat_c, og_c, feat_x, og_x, row(ml_norm_g[i]), head_mean)
        hy_x = _hyena(x0_x, z_x, filt_w, hbias, tmx, min(512, seq), 2)

        x = _out_ffn(x, att_x, ml_x, hy_x, w_out_l, mx[2], row(norm2_g[i]), mx[4], mx[3], mx[5],
                     wup, fcw, fcb, wdn, row(final_norm_g), tmx, last)

        if not last:
            att_c = _attention(q_c, k_c, v_c, None, None, tmc, tmc)
            hy_c = _hyena(x0_c, z_c, filt_w, hbias, tmc, tmc, 2)
            ctx = _out_ffn(ctx, att_c, ml_c, hy_c, w_out_l, mc[2], row(norm2_g[i]), mc[4], mc[3],
                           mc[5], wup, fcw, fcb, wdn, row(final_norm_g), tmc, False)
    return x
```

```python
import functools
import math

import numpy as np
import jax
import jax.numpy as jnp
from jax import lax
from jax.experimental import pallas as pl
from jax.experimental.pallas import tpu as pltpu

F32 = jnp.float32
BF16 = jnp.bfloat16
HIGHEST = lax.Precision.HIGHEST

D_MODEL = 1024
DEPTH = 2
GRID_W = 64
EPS = 1e-6
MLA_HEADS = 8
MLA_NOPE = 64
MLA_ROPE = 32
MLA_V = 64
MLA_Q_RANK = 384
MLA_KV_RANK = 256
ROPE_THETA = 10000.0
ML_HEADS = 4
ML_HEAD_DIM = 64
ML_WIDTH = 256
HY_WIDTH = 256
HY_BANDS = 16
HY_DECAY_TARGET = 1e-2
HY_FAST_PCT = 0.3
HY_SLOW_PCT = 1.5
HY_SHIFT = 0.05
N_MLA_IN = MLA_Q_RANK + MLA_KV_RANK + MLA_ROPE
N_ML_IN = 3 * ML_WIDTH + 4 * ML_HEADS
D_FF = 2816

LANES = 128
HEAD_PAD = 128
SEG = 768
ROPE_LANE = 64
GATE_LANE = 32
ML_CHUNK = 128
FF_CHUNK = 256
VMEM_LIMIT = 56 * 1024 * 1024


def _cparams(sem):
    return pltpu.CompilerParams(dimension_semantics=sem, vmem_limit_bytes=VMEM_LIMIT)


def _lane_iota(shape):
    return lax.broadcasted_iota(jnp.int32, shape, len(shape) - 1)


def _dot_split3(x, rhs):
    out = None
    for _ in range(3):
        piece = x.astype(BF16)
        term = jnp.dot(piece, rhs, preferred_element_type=F32)
        out = term if out is None else out + term
        x = x - piece.astype(F32)
    return out


def _swap_rope_halves(x):
    width = x.shape[-1]
    lane = _lane_iota(x.shape) % HEAD_PAD
    lo = pltpu.roll(x, width - MLA_ROPE // 2, x.ndim - 1)
    hi = pltpu.roll(x, MLA_ROPE // 2, x.ndim - 1)
    return jnp.where(lane < ROPE_LANE + MLA_ROPE // 2, lo, hi)


def _attn_kernel(*refs, tk, n_kx):
    if n_kx:
        q_ref, kc_ref, vc_ref, kx_ref, vx_ref, o_ref = refs
    else:
        q_ref, kc_ref, vc_ref, o_ref = refs
    tq = q_ref.shape[1]
    nt_dims = (((1,), (1,)), ((), ()))
    qs = [q_ref[0, :, hh * HEAD_PAD:(hh + 1) * HEAD_PAD] for hh in range(2)]

    def step(q, k, v, carry):
        m, l, acc = carry
        s = lax.dot_general(q, k, nt_dims, preferred_element_type=F32)
        m_new = jnp.maximum(m, jnp.max(s, axis=-1, keepdims=True))
        p = jnp.exp2(s - m_new)
        alpha = jnp.exp2(m - m_new)
        part = p[:, :LANES]
        for j in range(1, p.shape[1] // LANES):
            part = part + p[:, j * LANES:(j + 1) * LANES]
        l = alpha * l + part
        acc = alpha * acc + jnp.dot(p.astype(BF16), v, preferred_element_type=F32)
        return m_new, l, acc

    def both_heads(k2, v2, carry):
        return tuple(step(qs[hh], k2[:, hh * HEAD_PAD:(hh + 1) * HEAD_PAD], v2, carry[hh])
                     for hh in range(2))

    init = (jnp.full((tq, 1), -jnp.inf, F32), jnp.zeros((tq, LANES), F32),
            jnp.zeros((tq, 2 * MLA_V), F32))
    carry = both_heads(kc_ref[0], vc_ref[0], (init, init))
    if n_kx:
        def body(i, c):
            rows = pl.ds(pl.multiple_of(i * tk, tk), tk)
            return both_heads(kx_ref[0, rows, :], vx_ref[0, rows, :], c)
        carry = lax.fori_loop(0, n_kx, body, carry, unroll=True)
    outs = [acc / jnp.sum(l, axis=-1, keepdims=True) for _, l, acc in carry]
    lane = _lane_iota(outs[0].shape)
    o_ref[0] = jnp.where(lane < MLA_V, outs[0], outs[1]).astype(o_ref.dtype)


def _attention(q, kc, vc, kx, vx, tq, tk):
    b, lq, _ = q.shape
    lc = kc.shape[1]
    n_pairs = MLA_HEADS // 2
    in_specs = [
        pl.BlockSpec((1, tq, 2 * HEAD_PAD), lambda i, p, t: (i, t, p)),
        pl.BlockSpec((1, lc, 2 * HEAD_PAD), lambda i, p, t: (i, 0, p)),
        pl.BlockSpec((1, lc, 2 * MLA_V), lambda i, p, t: (i, 0, p)),
    ]
    args = [q, kc, vc]
    n_kx = 0
    if kx is not None:
        lx = kx.shape[1]
        n_kx = lx // tk
        in_specs += [
            pl.BlockSpec((1, lx, 2 * HEAD_PAD), lambda i, p, t: (i, 0, p)),
            pl.BlockSpec((1, lx, 2 * MLA_V), lambda i, p, t: (i, 0, p)),
        ]
        args += [kx, vx]
    return pl.pallas_call(
        functools.partial(_attn_kernel, tk=tk, n_kx=n_kx),
        grid=(b, n_pairs, lq // tq),
        in_specs=in_specs,
        out_specs=pl.BlockSpec((1, tq, 2 * MLA_V), lambda i, p, t: (i, t, p)),
        out_shape=jax.ShapeDtypeStruct((b, lq, MLA_HEADS * MLA_V), BF16),
        compiler_params=_cparams(("parallel", "parallel", "arbitrary")),
        name="attention",
    )(*args)


def _halo_specs(seq, tm, halo, width):
    r = tm // halo
    last = seq // halo - 1
    prev = pl.BlockSpec((1, halo, width), lambda i, t: (i, jnp.maximum(t * r - 1, 0), 0))
    nxt = pl.BlockSpec((1, halo, width), lambda i, t: (i, jnp.minimum((t + 1) * r, last), 0))
    return prev, nxt


def _conv3(ext_ref, halo, tm, w, b):
    return (b + w[0:1] * ext_ref[pl.ds(halo - 1, tm), :]
            + w[1:2] * ext_ref[pl.ds(halo, tm), :]
            + w[2:3] * ext_ref[pl.ds(halo + 1, tm), :])


def _log_sigmoid(x):
    return jnp.minimum(x, 0.0) - jnp.log1p(jnp.exp(-jnp.abs(x)))


def _mixer_in_kernel(xp_ref, x_ref, xn_ref, g_ref, sc_ref, sh_ref, w_ref,
                     gq_ref, gkv_ref, wq_ref, wqs_ref, wk_ref, wv_ref, cos_ref, sin_ref,
                     mcw_ref, mcb_ref, mwq_ref, mwk_ref, gb_ref, hcw_ref, hcb_ref,
                     q_ref, k_ref, v_ref, mk_ref, mqt_ref, mvt_ref, pc_ref, pr_ref, og_ref,
                     x0_ref, z_ref, ext_scr, *, halo):
    tm = x_ref.shape[1]
    t = pl.program_id(1)
    x_ext = jnp.concatenate([xp_ref[0], x_ref[0], xn_ref[0]], axis=0)
    ms = jnp.mean(x_ext * x_ext, axis=-1, keepdims=True)
    h = x_ext * lax.rsqrt(ms + EPS) * g_ref[...] * (1.0 + sc_ref[0]) + sh_ref[0]
    row = lax.broadcasted_iota(jnp.int32, (tm + 2 * halo, 1), 0)
    outside = ((row < halo) & (t == 0)) | ((row >= tm + halo) & (t == pl.num_programs(1) - 1))
    hb = jnp.where(outside, 0.0, h).astype(BF16)
    for i in range(3):
        ext_scr[:, i * SEG:(i + 1) * SEG] = jnp.dot(hb, w_ref[:, i * SEG:(i + 1) * SEG],
                                                    preferred_element_type=F32)
    inner = pl.ds(halo, tm)

    cos = cos_ref[...]
    sin = sin_ref[...]
    scale = (MLA_NOPE + MLA_ROPE) ** -0.5 * math.log2(math.e)
    ql = ext_scr[inner, 0:MLA_Q_RANK]
    qn = ql * lax.rsqrt(jnp.mean(ql * ql, axis=-1, keepdims=True) + EPS) * gq_ref[...]
    qn = qn.astype(BF16)
    cos8 = jnp.concatenate([cos] * MLA_HEADS, axis=-1)
    sin8 = jnp.concatenate([sin] * MLA_HEADS, axis=-1)
    q = (jnp.dot(qn, wq_ref[...], preferred_element_type=F32) * cos8
         + jnp.dot(qn, wqs_ref[...], preferred_element_type=F32) * sin8) * scale
    q_ref[0] = q.astype(q_ref.dtype)
    kvl = ext_scr[inner, MLA_Q_RANK:MLA_Q_RANK + MLA_KV_RANK]
    kvn = kvl * lax.rsqrt(jnp.mean(kvl * kvl, axis=-1, keepdims=True) + EPS) * gkv_ref[...]
    kvb = kvn.astype(BF16)
    kn = jnp.dot(kvb, wk_ref[...], preferred_element_type=F32)
    v_ref[0] = jnp.dot(kvb, wv_ref[...], preferred_element_type=F32).astype(v_ref.dtype)
    blk = ext_scr[inner, SEG - LANES:SEG]
    kr = jnp.where(_lane_iota(blk.shape) < MLA_ROPE, blk, 0.0)
    kr = pltpu.roll(kr, ROPE_LANE, 1)
    kr = kr * cos + _swap_rope_halves(kr) * sin
    k_ref[0] = (kn + jnp.concatenate([kr] * MLA_HEADS, axis=-1)).astype(k_ref.dtype)

    n_chunk = tm // ML_CHUNK
    ml0 = SEG
    u = jax.nn.silu(_conv3(ext_scr.at[:, ml0:ml0 + ML_WIDTH], halo, tm, mcw_ref[...],
                           mcb_ref[...])).astype(BF16)
    mk_ref[0] = jnp.dot(u, mwk_ref[...], preferred_element_type=F32).astype(mk_ref.dtype)
    q_t = jnp.dot(u, mwq_ref[...], preferred_element_type=F32).T
    v_t = ext_scr[inner, ml0 + ML_WIDTH:ml0 + 2 * ML_WIDTH].T
    og_ref[0] = ext_scr[inner, ml0 + 2 * ML_WIDTH:ml0 + 3 * ML_WIDTH].astype(og_ref.dtype)
    xg = blk + gb_ref[...]
    lane = _lane_iota(xg.shape) - GATE_LANE
    is_gate = (lane >= 0) & (lane < 4 * ML_HEADS)
    is_forget = is_gate & ((lane // ML_HEADS) % 2 == 1)
    packed = jnp.where(is_forget, _log_sigmoid(xg), jnp.where(is_gate, xg, 0.0))
    lane_c = _lane_iota((ML_CHUNK, LANES)) - GATE_LANE
    fwd_lane = (lane_c >= ML_HEADS) & (lane_c < 2 * ML_HEADS)
    bwd_lane = (lane_c >= 3 * ML_HEADS) & (lane_c < 4 * ML_HEADS)
    r_idx = lax.broadcasted_iota(jnp.int32, (ML_CHUNK, ML_CHUNK), 0)
    c_idx = lax.broadcasted_iota(jnp.int32, (ML_CHUNK, ML_CHUNK), 1)
    lower = (c_idx <= r_idx).astype(BF16)
    for j in range(n_chunk):
        cols = slice(j * ML_CHUNK, (j + 1) * ML_CHUNK)
        part = packed[cols, :]
        prefix = None
        rest = part
        for _ in range(3):
            piece = rest.astype(BF16)
            term = jnp.dot(lower, piece, preferred_element_type=F32)
            prefix = term if prefix is None else prefix + term
            rest = rest - piece.astype(F32)
        suffix = prefix[ML_CHUNK - 1:ML_CHUNK, :] - prefix + part
        part = jnp.where(fwd_lane, prefix, jnp.where(bwd_lane, suffix, part))
        pc_ref[0, cols, :] = part
        pr_ref[0, j] = part.T
        mqt_ref[0, j] = q_t[:, cols].astype(mqt_ref.dtype)
        mvt_ref[0, j] = v_t[:, cols].astype(mvt_ref.dtype)

    hy0 = 2 * SEG
    uh = _conv3(ext_scr.at[:, hy0:hy0 + SEG], halo, tm, hcw_ref[...], hcb_ref[...])
    x0_ref[0] = uh[:, :HY_WIDTH].astype(x0_ref.dtype)
    z_ref[...] = (uh[:, HY_WIDTH:2 * HY_WIDTH] * uh[:, 2 * HY_WIDTH:]).astype(z_ref.dtype)


def _mixer_in(x, g, sc, sh, w_in, gq, gkv, wq, wqs, wk, wv, cos, sin, mcw, mcb, mwq, mwk, gate_b,
              hcw, hcb, tm):
    b, seq, d = x.shape
    halo = 8
    bm = sc.shape[0]
    mod_map = (lambda i, t: (i, 0, 0)) if bm > 1 else (lambda i, t: (0, 0, 0))
    prev, nxt = _halo_specs(seq, tm, halo, d)
    const = lambda i, t: (0, 0)
    whole = lambda a: pl.BlockSpec(a.shape, const, pipeline_mode=pl.Buffered(1))
    tok = lambda width: pl.BlockSpec((1, tm, width), lambda i, t: (i, t, 0))
    n_chunk = tm // ML_CHUNK
    chunked = lambda rows: pl.BlockSpec((1, n_chunk, rows, ML_CHUNK), lambda i, t: (i, t, 0, 0))
    hw = MLA_HEADS * HEAD_PAD
    vw = MLA_HEADS * MLA_V
    w = ML_WIDTH
    consts = (g, None, None, w_in, gq, gkv, wq, wqs, wk, wv)
    in_specs = [prev, tok(d), nxt]
    for a in consts:
        in_specs.append(pl.BlockSpec((1, 1, d), mod_map) if a is None else whole(a))
    in_specs += [pl.BlockSpec((tm, HEAD_PAD), lambda i, t: (t, 0))] * 2
    tail = (mcw, mcb, mwq, mwk, gate_b, hcw, hcb)
    in_specs += [whole(a) for a in tail]
    outs = pl.pallas_call(
        functools.partial(_mixer_in_kernel, halo=halo),
        grid=(b, seq // tm),
        in_specs=in_specs,
        out_specs=[
            tok(hw), tok(hw), tok(vw),
            tok(w), chunked(w), chunked(w), tok(LANES), chunked(LANES), tok(w),
            tok(HY_WIDTH),
            pl.BlockSpec((tm, HY_WIDTH), lambda i, t: (t, i)),
        ],
        out_shape=[
            jax.ShapeDtypeStruct((b, seq, hw), BF16),
            jax.ShapeDtypeStruct((b, seq, hw), BF16),
            jax.ShapeDtypeStruct((b, seq, vw), BF16),
            jax.ShapeDtypeStruct((b, seq, w), BF16),
            jax.ShapeDtypeStruct((b, seq // ML_CHUNK, w, ML_CHUNK), BF16),
            jax.ShapeDtypeStruct((b, seq // ML_CHUNK, w, ML_CHUNK), BF16),
            jax.ShapeDtypeStruct((b, seq, LANES), F32),
            jax.ShapeDtypeStruct((b, seq // ML_CHUNK, LANES, ML_CHUNK), F32),
            jax.ShapeDtypeStruct((b, seq, w), BF16),
            jax.ShapeDtypeStruct((b, seq, HY_WIDTH), BF16),
            jax.ShapeDtypeStruct((seq, b * HY_WIDTH), BF16),
        ],
        scratch_shapes=[pltpu.VMEM((tm + 2 * halo, 3 * SEG), F32)],
        compiler_params=_cparams(("parallel", "arbitrary")),
        name="mixer_in",
    )(x, x, x, g, sc, sh, w_in, gq, gkv, wq, wqs, wk, wv, cos, sin, *tail)
    return outs[0:3], outs[3:8], outs[8], outs[9:11]


def _ml_chunk(k, q_t, v_t, p_col, p_row, state_ref, m_ref, direction):
    t_len = k.shape[0]
    half = ML_HEAD_DIM
    s_idx = lax.broadcasted_iota(jnp.int32, (t_len, t_len), 0)
    t_idx = lax.broadcasted_iota(jnp.int32, (t_len, t_len), 1)
    keep = (s_idx <= t_idx) if direction == 0 else (s_idx >= t_idx)
    row_p = lax.broadcasted_iota(jnp.int32, (2 * half, t_len), 0)
    last = t_len - 1 if direction == 0 else 0
    outs = []
    for h in range(ML_HEADS):
        pair, odd = divmod(h, 2)
        sl = slice(pair * 2 * half, (pair + 1) * 2 * half)
        mine = (row_p >= half) if odd else (row_p < half)
        one_row = 0 if odd else half
        li_l = GATE_LANE + 2 * direction * ML_HEADS + h
        cu_l = li_l + ML_HEADS
        b_c = p_col[:, li_l:li_l + 1] - p_col[:, cu_l:cu_l + 1]
        li_r = p_row[li_l:li_l + 1, :]
        cu_r = p_row[cu_l:cu_l + 1, :]
        m = m_ref[h]
        state = state_ref[h]

        b_m = jnp.where(keep, b_c, -jnp.inf)
        m_out = cu_r + jnp.maximum(m, jnp.max(b_m, axis=0, keepdims=True))
        kh = k[:, sl]
        qh_t = jnp.where(mine, q_t[sl, :], 0).astype(BF16)
        vh_t = jnp.where(mine, v_t[sl, :].astype(F32), (row_p == one_row).astype(F32))
        s_t = jnp.dot(kh, qh_t, preferred_element_type=F32) * jnp.exp(b_m + (cu_r - m_out))
        a = jnp.exp(cu_r + m - m_out)
        lhs = jnp.concatenate([vh_t.astype(BF16), state.astype(BF16)], axis=1)
        rhs = jnp.concatenate([s_t.astype(BF16), (qh_t.astype(F32) * a).astype(BF16)], axis=0)
        tot = jnp.dot(lhs, rhs, preferred_element_type=F32)
        den = jnp.maximum(jnp.abs(tot[one_row:one_row + 1, :]), jnp.exp(-m_out))
        outs.append(tot * (1.0 / den))

        cum_last = cu_r[:, last:last + 1]
        g = cum_last - cu_r + li_r
        m_new = jnp.maximum(cum_last + m, jnp.max(g, axis=1, keepdims=True))
        wts = jnp.exp(g - m_new)
        decay = jnp.exp(cum_last + m - m_new)
        upd = jnp.dot((vh_t * wts).astype(BF16), kh, preferred_element_type=F32)
        state_ref[h] = decay * state + upd
        m_ref[h] = m_new
    pairs = []
    for pair in range(ML_HEADS // 2):
        pairs.append(jnp.where(row_p < half, outs[2 * pair], outs[2 * pair + 1]))
    return jnp.concatenate(pairs, axis=0)


def _ml_scan_kernel(kc_ref, qtc_ref, vtc_ref, pcc_ref, prc_ref, oc_ref,
                    kx_ref, qtx_ref, vtx_ref, pcx_ref, prx_ref, ox_ref, ng_ref, bd_ref,
                    hx_ref, hc_ref, fx_scr, bx_scr, fc_scr, bc_scr, state_scr, m_scr):
    t_len = ML_CHUNK
    n_c = kc_ref.shape[1] // t_len
    n_x = kx_ref.shape[1] // t_len
    state_scr[...] = jnp.zeros_like(state_scr)
    m_scr[...] = jnp.zeros_like(m_scr)

    def rows_of(i):
        return pl.ds(pl.multiple_of(i * t_len, t_len), t_len)

    def scan(refs, scrs, n):
        k_ref, qt_ref, vt_ref, pc_ref, pr_ref = refs

        def body(j, _):
            for direction, i in ((0, j), (1, n - 1 - j)):
                rows = rows_of(i)
                scrs[direction][i] = _ml_chunk(
                    k_ref[0, rows, :], qt_ref[0, i], vt_ref[0, i], pc_ref[0, rows, :],
                    pr_ref[0, i], state_scr.at[direction], m_scr.at[direction], direction)
            return 0

        lax.fori_loop(0, n, body, 0, unroll=2)

    def finish(scrs, o_ref, out_ref, n):
        def body(i, _):
            rows = rows_of(i)
            h = (scrs[0][i] + scrs[1][i]).T
            y = h * jax.nn.sigmoid(o_ref[0, rows, :].astype(F32))
            ms = _dot_split3(y * y, bd_ref[...])
            out_ref[0, rows, :] = (y * lax.rsqrt(ms + EPS) * ng_ref[...]).astype(out_ref.dtype)
            return 0

        lax.fori_loop(0, n, body, 0, unroll=2)

    scan((kc_ref, qtc_ref, vtc_ref, pcc_ref, prc_ref), (fc_scr, bc_scr), n_c)
    scan((kx_ref, qtx_ref, vtx_ref, pcx_ref, prx_ref), (fx_scr, bx_scr), n_x)
    finish((fc_scr, bc_scr), oc_ref, hc_ref, n_c)
    finish((fx_scr, bx_scr), ox_ref, hx_ref, n_x)


def _ml_scan(feat_c, gate_c, feat_x, gate_x, norm_g, bd):
    b, lc, w = feat_c[0].shape
    lx = feat_x[0].shape[1]

    def seq_specs(n):
        n_chunk = n // ML_CHUNK
        return [
            pl.BlockSpec((1, n, w), lambda i: (i, 0, 0)),
            pl.BlockSpec((1, n_chunk, w, ML_CHUNK), lambda i: (i, 0, 0, 0)),
            pl.BlockSpec((1, n_chunk, w, ML_CHUNK), lambda i: (i, 0, 0, 0)),
            pl.BlockSpec((1, n, LANES), lambda i: (i, 0, 0)),
            pl.BlockSpec((1, n_chunk, LANES, ML_CHUNK), lambda i: (i, 0, 0, 0)),
            pl.BlockSpec((1, n, w), lambda i: (i, 0, 0)),
        ]

    const = lambda i: (0, 0)
    return pl.pallas_call(
        _ml_scan_kernel,
        grid=(b,),
        in_specs=seq_specs(lc) + seq_specs(lx) + [
            pl.BlockSpec((1, w), const),
            pl.BlockSpec((w, w), const),
        ],
        out_specs=[
            pl.BlockSpec((1, lx, w), lambda i: (i, 0, 0)),
            pl.BlockSpec((1, lc, w), lambda i: (i, 0, 0)),
        ],
        out_shape=[
            jax.ShapeDtypeStruct((b, lx, w), BF16),
            jax.ShapeDtypeStruct((b, lc, w), BF16),
        ],
        scratch_shapes=[
            pltpu.VMEM((lx // ML_CHUNK, w, ML_CHUNK), F32),
            pltpu.VMEM((lx // ML_CHUNK, w, ML_CHUNK), F32),
            pltpu.VMEM((lc // ML_CHUNK, w, ML_CHUNK), F32),
            pltpu.VMEM((lc // ML_CHUNK, w, ML_CHUNK), F32),
            pltpu.VMEM((2, ML_HEADS, 2 * ML_HEAD_DIM, 2 * ML_HEAD_DIM), F32),
            pltpu.VMEM((2, ML_HEADS, 1, 1), F32),
        ],
        compiler_params=_cparams(("parallel",)),
        name="ml_scan",
    )(*feat_c, gate_c, *feat_x, gate_x, norm_g, bd)


@functools.lru_cache(maxsize=None)
def _dft_mats(seq, kt):
    n_fft = 2 * seq
    k = np.arange(seq, dtype=np.int64)[:, None]
    n = np.arange(seq, dtype=np.int64)[None, :]
    ang = (2.0 * np.pi / n_fft) * ((k * n) % n_fft).astype(np.float64)
    cos = np.cos(ang)
    msin = -np.sin(ang)
    msin[0, :] = 1.0 - 2.0 * (np.arange(seq) % 2)
    fwd = np.stack([cos.reshape(seq // kt, kt, seq), msin.reshape(seq // kt, kt, seq)], axis=1)
    fwd = fwd.reshape(2 * seq, seq)
    wgt = np.full((seq, 1), 2.0 / n_fft)
    wgt[0, 0] = 1.0 / n_fft
    inv = np.stack([(cos * wgt).reshape(seq // kt, kt, seq),
                    (msin * wgt).reshape(seq // kt, kt, seq)], axis=1)
    inv = inv.reshape(2 * seq, seq).T
    return jnp.asarray(fwd, dtype=BF16), jnp.asarray(np.ascontiguousarray(inv), dtype=BF16)


def _dft_fwd_kernel(f_ref, z_ref, *rest, kt, reps):
    zz = jnp.dot(f_ref[...], z_ref[...], preferred_element_type=F32)
    if not rest[1:]:
        rest[0][...] = zz
        return
    h_ref, l1_ref, y_ref = rest
    w = HY_WIDTH
    inv_l1 = 1.0 / l1_ref[...]
    hc, hs = h_ref[:kt, :], h_ref[kt:, :]
    first = (lax.broadcasted_iota(jnp.int32, (kt, w), 0) == 0) & (pl.program_id(0) == 0)
    hre = (hc[:, :w] + hc[:, w:]) * inv_l1
    him = jnp.where(first, 0.0, (hs[:, :w] - hs[:, w:]) * inv_l1)
    hre2 = jnp.where(first, (hs[:, :w] + hs[:, w:]) * inv_l1, hre)
    tile = lambda r: jnp.concatenate([r] * reps, axis=-1)
    hre, him, hre2 = tile(hre), tile(him), tile(hre2)
    zre, zim = zz[:kt], zz[kt:]
    y_ref[:kt, :] = (zre * hre - zim * him).astype(y_ref.dtype)
    y_ref[kt:, :] = (zre * him + zim * hre2).astype(y_ref.dtype)


def _dft_fwd(fwd, z, spectrum, kt, cb):
    seq = z.shape[0]
    ncol = z.shape[1]
    in_specs = [
        pl.BlockSpec((2 * kt, seq), lambda i, j: (i, 0)),
        pl.BlockSpec((seq, cb), lambda i, j: (0, j)),
    ]
    args = [fwd, z]
    out_dtype = F32
    if spectrum is not None:
        in_specs += [pl.BlockSpec((2 * kt, 2 * HY_WIDTH), lambda i, j: (i, 0)),
                     pl.BlockSpec((1, HY_WIDTH), lambda i, j: (0, 0))]
        args += list(spectrum)
        out_dtype = BF16
    return pl.pallas_call(
        functools.partial(_dft_fwd_kernel, kt=kt, reps=cb // HY_WIDTH),
        grid=(seq // kt, ncol // cb),
        in_specs=in_specs,
        out_specs=pl.BlockSpec((2 * kt, cb), lambda i, j: (i, j)),
        out_shape=jax.ShapeDtypeStruct((2 * seq, ncol), out_dtype),
        compiler_params=_cparams(("parallel", "arbitrary")),
        name="dft_fwd",
    )(*args)


def _dft_inv_kernel(g_ref, y_ref, z_ref, x0_ref, bias_ref, o_ref, *, nb):
    y = jnp.dot(g_ref[...], y_ref[...], preferred_element_type=F32)
    bias = bias_ref[...]
    for bb in range(nb):
        cols = slice(bb * HY_WIDTH, (bb + 1) * HY_WIDTH)
        yb = y[:, cols] + z_ref[:, cols].astype(F32) * bias
        o_ref[bb] = (x0_ref[bb].astype(F32) * yb).astype(o_ref.dtype)


def _dft_inv(inv, y, z, x0, bias, tm, nb):
    b, seq, _ = x0.shape
    cb = nb * HY_WIDTH
    return pl.pallas_call(
        functools.partial(_dft_inv_kernel, nb=nb),
        grid=(seq // tm, b // nb),
        in_specs=[
            pl.BlockSpec((tm, 2 * seq), lambda t, j: (t, 0)),
            pl.BlockSpec((2 * seq, cb), lambda t, j: (0, j)),
            pl.BlockSpec((tm, cb), lambda t, j: (t, j)),
            pl.BlockSpec((nb, tm, HY_WIDTH), lambda t, j: (j, t, 0)),
            pl.BlockSpec((1, HY_WIDTH), lambda t, j: (0, 0)),
        ],
        out_specs=pl.BlockSpec((nb, tm, HY_WIDTH), lambda t, j: (j, t, 0)),
        out_shape=jax.ShapeDtypeStruct((b, seq, HY_WIDTH), BF16),
        compiler_params=_cparams(("parallel", "arbitrary")),
        name="dft_inv",
    )(inv, y, z, x0, bias)


@functools.lru_cache(maxsize=None)
def _filter_features(seq):
    t = np.linspace(0.0, 1.0, seq)[:, None]
    omega = 2.0 * math.pi * np.arange(seq) / seq
    bands = np.linspace(1e-4, HY_BANDS - 1, HY_BANDS)
    ang = omega[:, None] * bands[None, :]
    feats = np.zeros((seq, LANES), np.float32)
    feats[:, :1 + 2 * HY_BANDS] = np.concatenate([t, np.cos(ang), -np.sin(ang)], axis=-1)
    deltas = np.abs(np.linspace(math.log(HY_DECAY_TARGET) / HY_SLOW_PCT,
                                math.log(HY_DECAY_TARGET) / HY_FAST_PCT, HY_WIDTH))
    return jnp.asarray(feats), jnp.asarray(deltas[None, :].astype(np.float32))


def _filter_kernel(z_ref, dl_ref, w1_ref, b1_ref, w2_ref, b2_ref, w3_ref, fr_ref, taps_ref,
                   l1_ref):
    t = pl.program_id(0)
    z = z_ref[...]
    fr = fr_ref[...]
    dot = functools.partial(jnp.dot, precision=HIGHEST, preferred_element_type=F32)
    hdn = jnp.sin(fr * (dot(z, w1_ref[...]) + b1_ref[...]))
    hdn = jnp.sin(fr * (dot(hdn, w2_ref[...]) + b2_ref[...]))
    filt = dot(hdn, w3_ref[...])
    window = jnp.exp(-z[:, 0:1] * dl_ref[...]) + HY_SHIFT
    h_fwd = filt[:, :HY_WIDTH] * window
    h_bwd = filt[:, HY_WIDTH:] * window
    first = (lax.broadcasted_iota(jnp.int32, h_bwd.shape, 0) == 0) & (t == 0)
    h_bwd = jnp.where(first, 0.0, h_bwd)
    taps_ref[...] = jnp.concatenate([h_fwd, h_bwd], axis=-1).astype(taps_ref.dtype)
    part = jnp.sum(jnp.abs(h_fwd) + jnp.abs(h_bwd), axis=0, keepdims=True)

    @pl.when(t == 0)
    def _():
        l1_ref[...] = part

    @pl.when(t > 0)
    def _():
        l1_ref[...] += part


def _hyena_filter(seq, tm, w1, b1, w2, b2, w3, sin_freq):
    feats, deltas = _filter_features(seq)
    hid = w2.shape[0]
    w1p = jnp.zeros((LANES, hid), F32).at[:w1.shape[0]].set(w1)
    const = lambda t: (0, 0)
    row = lambda v: v.reshape(1, -1)
    return pl.pallas_call(
        _filter_kernel,
        grid=(seq // tm,),
        in_specs=[
            pl.BlockSpec((tm, LANES), lambda t: (t, 0)),
            pl.BlockSpec((1, HY_WIDTH), const),
            pl.BlockSpec((LANES, hid), const),
            pl.BlockSpec((1, hid), const),
            pl.BlockSpec((hid, hid), const),
            pl.BlockSpec((1, hid), const),
            pl.BlockSpec((hid, 2 * HY_WIDTH), const),
            pl.BlockSpec((1, hid), const),
        ],
        out_specs=[
            pl.BlockSpec((tm, 2 * HY_WIDTH), lambda t: (t, 0)),
            pl.BlockSpec((1, HY_WIDTH), const),
        ],
        out_shape=[
            jax.ShapeDtypeStruct((seq, 2 * HY_WIDTH), BF16),
            jax.ShapeDtypeStruct((1, HY_WIDTH), F32),
        ],
        compiler_params=_cparams(("arbitrary",)),
        name="hy_filter",
    )(feats, deltas, w1p, row(b1), w2, row(b2), w3, row(sin_freq))


FFT_RADIX = 16


@functools.lru_cache(maxsize=None)
def _fft_consts(seq):
    n_fft = 2 * seq
    s_half = FFT_RADIX // 2
    f_len = n_fft // FFT_RADIX
    ks = np.arange(s_half + 1)
    theta = 2.0 * np.pi * np.outer(ks, np.arange(s_half)) / FFT_RADIX
    wgt = np.where((ks == 0) | (ks == s_half), 1.0, 2.0)[:, None] / n_fft
    coef = np.concatenate([np.cos(theta), -np.sin(theta), wgt * np.cos(theta),
                           -wgt * np.sin(theta)], axis=1).astype(np.float32)
    f = np.arange(f_len, dtype=np.int64)
    fwd, inv = [], []
    for k0 in ks:
        k = k0 + FFT_RADIX * np.arange(f_len, dtype=np.int64)
        ang = (2.0 * np.pi / n_fft) * ((k[:, None] * f[None, :]) % n_fft)
        m_re, m_im = np.cos(ang), -np.sin(ang)
        fwd.append(np.concatenate([m_re, m_im], axis=0))
        inv.append(np.concatenate([m_re.T, m_im.T], axis=0))
    return (jnp.asarray(coef), jnp.asarray(np.stack(fwd), dtype=BF16),
            jnp.asarray(np.stack(inv), dtype=BF16))


def _fft_forward(coef_ref, ks, z_ref, fm_ref, f_len, width):
    s_half = FFT_RADIX // 2
    a_re = a_im = None
    for s in range(s_half):
        blk = z_ref[s * f_len:(s + 1) * f_len, :].astype(F32)
        t_re, t_im = coef_ref[ks, s] * blk, coef_ref[ks, s_half + s] * blk
        a_re = t_re if a_re is None else a_re + t_re
        a_im = t_im if a_im is None else a_im + t_im
    p = jnp.dot(fm_ref[0], jnp.concatenate([a_re, a_im], axis=1).astype(BF16),
                preferred_element_type=F32)
    x_re = p[:f_len, :width] - p[f_len:, width:]
    x_im = p[f_len:, :width] + p[:f_len, width:]
    return x_re, x_im


def _fft_filter_kernel(coef_ref, taps_ref, fm_ref, o_ref, *, f_len):
    x_re, x_im = _fft_forward(coef_ref, pl.program_id(0), taps_ref, fm_ref, f_len,
                              taps_ref.shape[1])
    o_ref[0, :f_len, :] = x_re
    o_ref[0, f_len:, :] = x_im


def _fft_conv_kernel(coef_ref, z_ref, fm_ref, im_ref, h_ref, l1_ref, x0_ref, bias_ref, o_ref,
                     y_scr, zf_scr, *, f_len):
    ks = pl.program_id(1)
    s_half = FFT_RADIX // 2
    w = HY_WIDTH

    @pl.when(ks == 0)
    def _():
        zf = z_ref[...].astype(F32)
        zf_scr[...] = zf
        y_scr[...] = zf * bias_ref[...]

    x_re, x_im = _fft_forward(coef_ref, ks, zf_scr, fm_ref, f_len, w)
    inv_l1 = 1.0 / l1_ref[...]
    h_re = (h_ref[0, :f_len, :w] + h_ref[0, :f_len, w:]) * inv_l1
    h_im = (h_ref[0, f_len:, :w] - h_ref[0, f_len:, w:]) * inv_l1
    y_re = x_re * h_re - x_im * h_im
    y_im = x_re * h_im + x_im * h_re
    r = jnp.dot(im_ref[0], jnp.concatenate([y_re, y_im], axis=1).astype(BF16),
                preferred_element_type=F32)
    b_re = r[:f_len, :w] + r[f_len:, w:]
    b_im = r[:f_len, w:] - r[f_len:, :w]

    for s in range(s_half):
        rows = slice(s * f_len, (s + 1) * f_len)
        y_scr[rows, :] += (coef_ref[ks, 2 * s_half + s] * b_re
                           + coef_ref[ks, 3 * s_half + s] * b_im)

    @pl.when(ks == s_half)
    def _():
        o_ref[0] = (x0_ref[0].astype(F32) * y_scr[...]).astype(o_ref.dtype)


def _hyena_fft(z, x0, taps, l1, bias_d):
    b, seq, w = x0.shape
    coef, fwd_m, inv_m = _fft_consts(seq)
    n_ks = FFT_RADIX // 2 + 1
    f_len = 2 * seq // FFT_RADIX
    smem = pl.BlockSpec(memory_space=pltpu.SMEM)
    h_raw = pl.pallas_call(
        functools.partial(_fft_filter_kernel, f_len=f_len),
        grid=(n_ks,),
        in_specs=[smem,
                  pl.BlockSpec((seq, 2 * w), lambda k: (0, 0)),
                  pl.BlockSpec((1, 2 * f_len, f_len), lambda k: (k, 0, 0))],
        out_specs=pl.BlockSpec((1, 2 * f_len, 2 * w), lambda k: (k, 0, 0)),
        out_shape=jax.ShapeDtypeStruct((n_ks, 2 * f_len, 2 * w), F32),
        compiler_params=_cparams(("parallel",)),
        name="hy_fft_filter",
    )(coef, taps, fwd_m)
    return pl.pallas_call(
        functools.partial(_fft_conv_kernel, f_len=f_len),
        grid=(b, n_ks),
        in_specs=[smem,
                  pl.BlockSpec((seq, w), lambda i, k: (0, i)),
                  pl.BlockSpec((1, 2 * f_len, f_len), lambda i, k: (k, 0, 0)),
                  pl.BlockSpec((1, 2 * f_len, f_len), lambda i, k: (k, 0, 0)),
                  pl.BlockSpec((1, 2 * f_len, 2 * w), lambda i, k: (k, 0, 0)),
                  pl.BlockSpec((1, w), lambda i, k: (0, 0)),
                  pl.BlockSpec((1, seq, w), lambda i, k: (i, 0, 0)),
                  pl.BlockSpec((1, w), lambda i, k: (0, 0))],
        out_specs=pl.BlockSpec((1, seq, w), lambda i, k: (i, 0, 0)),
        out_shape=jax.ShapeDtypeStruct((b, seq, w), BF16),
        scratch_shapes=[pltpu.VMEM((seq, w), F32), pltpu.VMEM((seq, w), F32)],
        compiler_params=_cparams(("parallel", "arbitrary")),
        name="hy_fft_conv",
    )(coef, z, fwd_m, inv_m, h_raw, l1, x0, bias_d)


def _hyena(x0, z, filt_w, bias_d, tm, kt, nb):
    seq = x0.shape[1]
    taps, l1 = _hyena_filter(seq, tm, *filt_w)
    if 2 * seq // FFT_RADIX >= LANES:
        return _hyena_fft(z, x0, taps, l1, bias_d)
    fwd, inv = _dft_mats(seq, kt)
    raw = _dft_fwd(fwd, taps, None, kt, 2 * HY_WIDTH)
    y = _dft_fwd(fwd, z, (raw, l1), kt, nb * HY_WIDTH)
    return _dft_inv(inv, y, z, x0, bias_d, tm, nb)


def _ffn_kernel(xp_ref, x_ref, xn_ref, ap_ref, a_ref, an_ref, mp_ref, m_ref, mn_ref, hp_ref,
                h_ref, hn_ref, wo_ref, g1_ref, g_ref, sc_ref, sh_ref, gate_ref, wup_ref, cw_ref,
                cb_ref, wdn_ref, fg_ref, o_ref, xe_scr, u_scr, acc_scr, *, halo, mix_halo,
                final_norm):
    tm = x_ref.shape[1]
    d_ff = wdn_ref.shape[0]
    n_chunks = d_ff // FF_CHUNK
    t = pl.program_id(1)

    def with_halo(p_ref, c_ref, n_ref):
        prev = p_ref[0].astype(F32)[mix_halo - halo:, :]
        nxt = n_ref[0].astype(F32)[:halo, :]
        return jnp.concatenate([prev, c_ref[0].astype(F32), nxt], axis=0).astype(BF16)

    lo = 0
    mixed = None
    for refs in ((ap_ref, a_ref, an_ref), (mp_ref, m_ref, mn_ref), (hp_ref, h_ref, hn_ref)):
        width = refs[1].shape[2]
        part = jnp.dot(with_halo(*refs), wo_ref[lo:lo + width, :], preferred_element_type=F32)
        mixed = part if mixed is None else mixed + part
        lo += width
    x_ext = jnp.concatenate([xp_ref[0], x_ref[0], xn_ref[0]], axis=0) + g1_ref[0] * mixed

    def cols(c):
        return (slice(c * FF_CHUNK, (c + 1) * FF_CHUNK),
                slice(d_ff + c * FF_CHUNK, d_ff + (c + 1) * FF_CHUNK))

    def norm_mod(x):
        ms = jnp.mean(x * x, axis=-1, keepdims=True)
        return x * lax.rsqrt(ms + EPS) * g_ref[...] * (1.0 + sc_ref[0]) + sh_ref[0]

    row = lax.broadcasted_iota(jnp.int32, (tm + 2 * halo, 1), 0)
    outside = ((row < halo) & (t == 0)) | ((row >= tm + halo) & (t == pl.num_programs(1) - 1))
    xe_scr[...] = jnp.where(outside, 0.0, norm_mod(x_ext)).astype(BF16)
    x_mid = x_ext[halo:halo + tm, :]

    def up_proj(c, slot):
        xe_b = xe_scr[...]
        for half, sl in enumerate(cols(c)):
            u_scr[slot, :, half * FF_CHUNK:(half + 1) * FF_CHUNK] = jnp.dot(
                xe_b, wup_ref[:, sl], preferred_element_type=F32)

    up_proj(0, 0)
    for c in range(n_chunks):
        slot = c % 2
        if c + 1 < n_chunks:
            up_proj(c + 1, 1 - slot)
        gate_cols, val_cols = cols(c)
        cw = jnp.concatenate([cw_ref[:, gate_cols], cw_ref[:, val_cols]], axis=-1)
        cb = jnp.concatenate([cb_ref[:, gate_cols], cb_ref[:, val_cols]], axis=-1)
        uc = _conv3(u_scr.at[slot], halo, tm, cw, cb)
        act = jax.nn.silu(uc[:, :FF_CHUNK]) * uc[:, FF_CHUNK:]
        acc_scr[:, gate_cols] = act.astype(BF16)
    down = jnp.dot(acc_scr[...], wdn_ref[...], preferred_element_type=F32)
    y = x_mid + gate_ref[0] * down
    if final_norm:
        y = y * lax.rsqrt(jnp.mean(y * y, axis=-1, keepdims=True) + EPS) * fg_ref[...]
    o_ref[0] = y


def _out_ffn(x, att, ml, hy, w_out, gate1, g, sc, sh, gate, wup, cw, cb, wdn, final_g, tm,
             final_norm):
    b, seq, d = x.shape
    halo = 8
    mix_halo = 16
    bm = sc.shape[0]
    mod_map = (lambda i, t: (i, 0, 0)) if bm > 1 else (lambda i, t: (0, 0, 0))
    prev, nxt = _halo_specs(seq, tm, halo, d)
    const2 = lambda i, t: (0, 0)
    d_ff = wdn.shape[0]
    mix_specs, mix_args = [], []
    for a in (att, ml, hy):
        width = a.shape[2]
        m_prev, m_nxt = _halo_specs(seq, tm, mix_halo, width)
        mix_specs += [m_prev, pl.BlockSpec((1, tm, width), lambda i, t: (i, t, 0)), m_nxt]
        mix_args += [a, a, a]
    return pl.pallas_call(
        functools.partial(_ffn_kernel, halo=halo, mix_halo=mix_halo, final_norm=final_norm),
        grid=(b, seq // tm),
        in_specs=[
            prev,
            pl.BlockSpec((1, tm, d), lambda i, t: (i, t, 0)),
            nxt,
            *mix_specs,
            pl.BlockSpec((d, d), const2, pipeline_mode=pl.Buffered(1)),
            pl.BlockSpec((1, 1, d), mod_map),
            pl.BlockSpec((1, d), const2),
            pl.BlockSpec((1, 1, d), mod_map),
            pl.BlockSpec((1, 1, d), mod_map),
            pl.BlockSpec((1, 1, d), mod_map),
            pl.BlockSpec((d, 2 * d_ff), const2, pipeline_mode=pl.Buffered(1)),
            pl.BlockSpec((3, 2 * d_ff), const2),
            pl.BlockSpec((1, 2 * d_ff), const2),
            pl.BlockSpec((d_ff, d), const2, pipeline_mode=pl.Buffered(1)),
            pl.BlockSpec((1, d), const2),
        ],
        out_specs=pl.BlockSpec((1, tm, d), lambda i, t: (i, t, 0)),
        out_shape=jax.ShapeDtypeStruct((b, seq, d), F32),
        scratch_shapes=[
            pltpu.VMEM((tm + 2 * halo, d), BF16),
            pltpu.VMEM((2, tm + 2 * halo, 2 * FF_CHUNK), F32),
            pltpu.VMEM((tm, d_ff), BF16),
        ],
        compiler_params=_cparams(("parallel", "arbitrary")),
        name="out_ffn",
    )(x, x, x, *mix_args, w_out, gate1, g, sc, sh, gate, wup, cw, cb, wdn, final_g)


def _layout_w_in(w):
    d = w.shape[0]
    ml_lo = N_MLA_IN
    hy_lo = N_MLA_IN + N_ML_IN
    gates = w[:, ml_lo + 3 * ML_WIDTH:hy_lo]
    pad = jnp.zeros((d, LANES - MLA_ROPE - 4 * ML_HEADS), w.dtype)
    return jnp.concatenate([w[:, :N_MLA_IN], gates, pad,
                            w[:, ml_lo:ml_lo + 3 * ML_WIDTH], w[:, hy_lo:]], axis=1).astype(BF16)


def _layout_w_uq(w):
    r = w.shape[0]
    w = w.reshape(r, MLA_HEADS, MLA_NOPE + MLA_ROPE)
    pad = jnp.zeros((r, MLA_HEADS, HEAD_PAD - MLA_NOPE - MLA_ROPE), w.dtype)
    return jnp.concatenate([w, pad], axis=-1).reshape(r, MLA_HEADS * HEAD_PAD).astype(BF16)


def _layout_w_uq_swapped(w):
    r = w.shape[0]
    w = w.reshape(r, MLA_HEADS, MLA_NOPE + MLA_ROPE)
    half = MLA_ROPE // 2
    lo = jnp.zeros((r, MLA_HEADS, ROPE_LANE), w.dtype)
    hi = jnp.zeros((r, MLA_HEADS, HEAD_PAD - ROPE_LANE - MLA_ROPE), w.dtype)
    out = jnp.concatenate([lo, w[..., MLA_NOPE + half:], w[..., MLA_NOPE:MLA_NOPE + half], hi],
                          axis=-1)
    return out.reshape(r, MLA_HEADS * HEAD_PAD).astype(BF16)


def _layout_w_ukv(w):
    r = w.shape[0]
    w = w.reshape(r, MLA_HEADS, MLA_NOPE + MLA_V)
    pad = jnp.zeros((r, MLA_HEADS, HEAD_PAD - MLA_NOPE), w.dtype)
    wk = jnp.concatenate([w[..., :MLA_NOPE], pad], axis=-1).reshape(r, MLA_HEADS * HEAD_PAD)
    wv = w[..., MLA_NOPE:].reshape(r, MLA_HEADS * MLA_V)
    return wk.astype(BF16), wv.astype(BF16)


def _block_diag(w):
    h, d, _ = w.shape
    eye = jnp.eye(h, dtype=w.dtype)
    return jnp.einsum('hde,hg->hdge', w, eye).reshape(h * d, h * d)


@functools.lru_cache(maxsize=None)
def _rope_tables(n_rows, rope):
    n_tok = n_rows * GRID_W
    cos_t = np.zeros((n_tok, HEAD_PAD), np.float32)
    sin_t = np.zeros((n_tok, HEAD_PAD), np.float32)
    cos_t[:, :ROPE_LANE + MLA_ROPE] = 1.0
    if rope:
        n_freq = MLA_ROPE // 4
        half = MLA_ROPE // 2
        inv = ROPE_THETA ** (-np.arange(n_freq, dtype=np.float64) / n_freq)
        row = np.repeat(np.arange(n_rows, dtype=np.float64), GRID_W)
        col = np.tile(np.arange(GRID_W, dtype=np.float64), n_rows)
        ang = np.concatenate([row[:, None] * inv, col[:, None] * inv], axis=-1)
        cos_t[:, ROPE_LANE:ROPE_LANE + half] = np.cos(ang)
        cos_t[:, ROPE_LANE + half:ROPE_LANE + MLA_ROPE] = np.cos(ang)
        sin_t[:, ROPE_LANE:ROPE_LANE + half] = -np.sin(ang)
        sin_t[:, ROPE_LANE + half:ROPE_LANE + MLA_ROPE] = np.sin(ang)
    return jnp.asarray(cos_t), jnp.asarray(sin_t)


def _ada_kernel(c_ref, w_ref, b_ref, o_ref):
    act = jax.nn.silu(c_ref[...])
    o_ref[0] = jnp.dot(act, w_ref[0], precision=HIGHEST, preferred_element_type=F32) + b_ref[0]


def _ada_mod(cvec, ada_w, ada_b):
    depth, d, n = ada_w.shape
    rows = cvec.shape[0]
    tn = n // 4
    return pl.pallas_call(
        _ada_kernel,
        grid=(depth, n // tn),
        in_specs=[
            pl.BlockSpec((rows, d), lambda l, j: (0, 0)),
            pl.BlockSpec((1, d, tn), lambda l, j: (l, 0, j)),
            pl.BlockSpec((1, 1, tn), lambda l, j: (l, 0, j)),
        ],
        out_specs=pl.BlockSpec((1, rows, tn), lambda l, j: (l, 0, j)),
        out_shape=jax.ShapeDtypeStruct((depth, rows, n), F32),
        compiler_params=_cparams(("parallel", "parallel")),
        name="ada_mod",
    )(cvec, ada_w, ada_b.reshape(depth, 1, n))


def kernel(x, c, ctx, c_ctx, ada_w, ada_b, norm1_g, norm2_g, w_in, mla_q_norm_g, mla_kv_norm_g,
           mla_w_uq, mla_w_ukv, ml_conv_w, ml_conv_b, ml_wq, ml_wk, ml_gate_b, ml_norm_g,
           hy_conv_w, hy_conv_b, hy_w1, hy_b1, hy_w2, hy_b2, hy_w3, hy_sin_freq, hy_bias_d,
           w_out, ffn_w_up, ffn_conv_w, ffn_conv_b, ffn_w_down, final_norm_g):
    b, seq, d = x.shape
    lc = ctx.shape[1]
    depth = ada_w.shape[0]
    tmx = min(512, seq)
    tmc = min(256, lc)
    cos_x, sin_x = _rope_tables(seq // GRID_W, True)
    cos_c, sin_c = _rope_tables(lc // GRID_W, False)
    head_mean = _block_diag(
        jnp.full((ML_HEADS, ML_HEAD_DIM, ML_HEAD_DIM), 1.0 / ML_HEAD_DIM, F32)).astype(BF16)
    row = lambda v: v.reshape(1, -1)

    n_cond = -(-(b + 1) // 8) * 8
    cvec = jnp.concatenate([c, c_ctx[None], jnp.zeros((n_cond - b - 1, d), c.dtype)], axis=0)
    mod = _ada_mod(cvec, ada_w, ada_b)

    for i in range(depth):
        last = i == depth - 1
        mx = [m[:, None, :] for m in jnp.split(mod[i, :b], 6, axis=-1)]
        mc = [m[:, None, :] for m in jnp.split(mod[i, b:b + 1], 6, axis=-1)]

        w_in_l = _layout_w_in(w_in[i])
        wq = _layout_w_uq(mla_w_uq[i])
        wk, wv = _layout_w_ukv(mla_w_ukv[i])
        wq_bd = _block_diag(ml_wq[i]).astype(BF16)
        wk_bd = (_block_diag(ml_wk[i]) * (ML_HEAD_DIM ** -0.5)).astype(BF16)
        gate_b = jnp.zeros((1, LANES), F32).at[0, GATE_LANE:GATE_LANE + 4 * ML_HEADS].set(
            ml_gate_b[i].reshape(-1))
        wup, wdn = ffn_w_up[i].astype(BF16), ffn_w_down[i].astype(BF16)
        fcw, fcb = ffn_conv_w[i], row(ffn_conv_b[i])
        w_out_l = w_out[i].astype(BF16)
        filt_w = (hy_w1[i], hy_b1[i], hy_w2[i], hy_b2[i], hy_w3[i], hy_sin_freq[i])
        g1 = row(norm1_g[i])

        gq, gkv = row(mla_q_norm_g[i]), row(mla_kv_norm_g[i])
        wqs = _layout_w_uq_swapped(mla_w_uq[i])
        mcw, mcb = ml_conv_w[i], row(ml_conv_b[i])
        hcw, hcb, hbias = hy_conv_w[i], row(hy_conv_b[i]), row(hy_bias_d[i])
        shared = (mcw, mcb, wq_bd, wk_bd, gate_b, hcw, hcb)
        (q_x, k_x, v_x), feat_x, og_x, (x0_x, z_x) = _mixer_in(
            x, g1, mx[1], mx[0], w_in_l, gq, gkv, wq, wqs, wk, wv, cos_x, sin_x, *shared, tmx)
        (q_c, k_c, v_c), feat_c, og_c, (x0_c, z_c) = _mixer_in(
            ctx, g1, mc[1], mc[0], w_in_l, gq, gkv, wq, wqs, wk, wv, cos_c, sin_c, *shared, tmc)

        att_x = _attention(q_x, k_c, v_c, k_x, v_x, min(1024, seq), min(256, seq))
        ml_x, ml_c = _ml_scan(feat_c, og_c, feat_x, og_x, row(ml_norm_g[i]), head_mean)
        hy_x = _hyena(x0_x, z_x, filt_w, hbias, tmx, min(512, seq), 2)

        x = _out_ffn(x, att_x, ml_x, hy_x, w_out_l, mx[2], row(norm2_g[i]), mx[4], mx[3], mx[5],
                     wup, fcw, fcb, wdn, row(final_norm_g), tmx, last)

        if not last:
            att_c = _attention(q_c, k_c, v_c, None, None, tmc, tmc)
            hy_c = _hyena(x0_c, z_c, filt_w, hbias, tmc, tmc, 2)
            ctx = _out_ffn(ctx, att_c, ml_c, hy_c, w_out_l, mc[2], row(norm2_g[i]), mc[4], mc[3],
                           mc[5], wup, fcw, fcb, wdn, row(final_norm_g), tmc, False)
    return x
```

```python
import functools
import math

import numpy as np
import jax
import jax.numpy as jnp
from jax import lax
from jax.experimental import pallas as pl
from jax.experimental.pallas import tpu as pltpu

F32 = jnp.float32
BF16 = jnp.bfloat16
HIGHEST = lax.Precision.HIGHEST

D_MODEL = 1024
DEPTH = 2
GRID_W = 64
EPS = 1e-6
MLA_HEADS = 8
MLA_NOPE = 64
MLA_ROPE = 32
MLA_V = 64
MLA_Q_RANK = 384
MLA_KV_RANK = 256
ROPE_THETA = 10000.0
ML_HEADS = 4
ML_HEAD_DIM = 64
ML_WIDTH = 256
HY_WIDTH = 256
HY_BANDS = 16
HY_DECAY_TARGET = 1e-2
HY_FAST_PCT = 0.3
HY_SLOW_PCT = 1.5
HY_SHIFT = 0.05
N_MLA_IN = MLA_Q_RANK + MLA_KV_RANK + MLA_ROPE
N_ML_IN = 3 * ML_WIDTH + 4 * ML_HEADS
D_FF = 2816

LANES = 128
HEAD_PAD = 128
SEG = 768
ROPE_LANE = 64
GATE_LANE = 32
ML_CHUNK = 128
FF_CHUNK = 256
VMEM_LIMIT = 56 * 1024 * 1024


def _cparams(sem):
    return pltpu.CompilerParams(dimension_semantics=sem, vmem_limit_bytes=VMEM_LIMIT)


def _lane_iota(shape):
    return lax.broadcasted_iota(jnp.int32, shape, len(shape) - 1)


def _dot_split3(x, rhs):
    out = None
    for _ in range(3):
        piece = x.astype(BF16)
        term = jnp.dot(piece, rhs, preferred_element_type=F32)
        out = term if out is None else out + term
        x = x - piece.astype(F32)
    return out


def _swap_rope_halves(x):
    width = x.shape[-1]
    lane = _lane_iota(x.shape) % HEAD_PAD
    lo = pltpu.roll(x, width - MLA_ROPE // 2, x.ndim - 1)
    hi = pltpu.roll(x, MLA_ROPE // 2, x.ndim - 1)
    return jnp.where(lane < ROPE_LANE + MLA_ROPE // 2, lo, hi)


def _attn_kernel(*refs, tk, n_kx):
    if n_kx:
        q_ref, kc_ref, vc_ref, kx_ref, vx_ref, o_ref = refs
    else:
        q_ref, kc_ref, vc_ref, o_ref = refs
    tq = q_ref.shape[1]
    nt_dims = (((1,), (1,)), ((), ()))
    qs = [q_ref[0, :, hh * HEAD_PAD:(hh + 1) * HEAD_PAD] for hh in range(2)]

    def step(q, k, v, carry):
        m, l, acc = carry
        s = lax.dot_general(q, k, nt_dims, preferred_element_type=F32)
        m_new = jnp.maximum(m, jnp.max(s, axis=-1, keepdims=True))
        p = jnp.exp2(s - m_new)
        alpha = jnp.exp2(m - m_new)
        part = p[:, :LANES]
        for j in range(1, p.shape[1] // LANES):
            part = part + p[:, j * LANES:(j + 1) * LANES]
        l = alpha * l + part
        acc = alpha * acc + jnp.dot(p.astype(BF16), v, preferred_element_type=F32)
        return m_new, l, acc

    def both_heads(k2, v2, carry):
        return tuple(step(qs[hh], k2[:, hh * HEAD_PAD:(hh + 1) * HEAD_PAD], v2, carry[hh])
                     for hh in range(2))

    init = (jnp.full((tq, 1), -jnp.inf, F32), jnp.zeros((tq, LANES), F32),
            jnp.zeros((tq, 2 * MLA_V), F32))
    carry = both_heads(kc_ref[0], vc_ref[0], (init, init))
    if n_kx:
        def body(i, c):
            rows = pl.ds(pl.multiple_of(i * tk, tk), tk)
            return both_heads(kx_ref[0, rows, :], vx_ref[0, rows, :], c)
        carry = lax.fori_loop(0, n_kx, body, carry, unroll=True)
    outs = [acc / jnp.sum(l, axis=-1, keepdims=True) for _, l, acc in carry]
    lane = _lane_iota(outs[0].shape)
    o_ref[0] = jnp.where(lane < MLA_V, outs[0], outs[1]).astype(o_ref.dtype)


def _attention(q, kc, vc, kx, vx, tq, tk):
    b, lq, _ = q.shape
    lc = kc.shape[1]
    n_pairs = MLA_HEADS // 2
    in_specs = [
        pl.BlockSpec((1, tq, 2 * HEAD_PAD), lambda i, p, t: (i, t, p)),
        pl.BlockSpec((1, lc, 2 * HEAD_PAD), lambda i, p, t: (i, 0, p)),
        pl.BlockSpec((1, lc, 2 * MLA_V), lambda i, p, t: (i, 0, p)),
    ]
    args = [q, kc, vc]
    n_kx = 0
    if kx is not None:
        lx = kx.shape[1]
        n_kx = lx // tk
        in_specs += [
            pl.BlockSpec((1, lx, 2 * HEAD_PAD), lambda i, p, t: (i, 0, p)),
            pl.BlockSpec((1, lx, 2 * MLA_V), lambda i, p, t: (i, 0, p)),
        ]
        args += [kx, vx]
    return pl.pallas_call(
        functools.partial(_attn_kernel, tk=tk, n_kx=n_kx),
        grid=(b, n_pairs, lq // tq),
        in_specs=in_specs,
        out_specs=pl.BlockSpec((1, tq, 2 * MLA_V), lambda i, p, t: (i, t, p)),
        out_shape=jax.ShapeDtypeStruct((b, lq, MLA_HEADS * MLA_V), BF16),
        compiler_params=_cparams(("parallel", "parallel", "arbitrary")),
        name="attention",
    )(*args)


def _halo_specs(seq, tm, halo, width):
    r = tm // halo
    last = seq // halo - 1
    prev = pl.BlockSpec((1, halo, width), lambda i, t: (i, jnp.maximum(t * r - 1, 0), 0))
    nxt = pl.BlockSpec((1, halo, width), lambda i, t: (i, jnp.minimum((t + 1) * r, last), 0))
    return prev, nxt


def _conv3(ext_ref, halo, tm, w, b):
    return (b + w[0:1] * ext_ref[pl.ds(halo - 1, tm), :]
            + w[1:2] * ext_ref[pl.ds(halo, tm), :]
            + w[2:3] * ext_ref[pl.ds(halo + 1, tm), :])


def _log_sigmoid(x):
    return jnp.minimum(x, 0.0) - jnp.log1p(jnp.exp(-jnp.abs(x)))


def _mixer_in_kernel(xp_ref, x_ref, xn_ref, g_ref, sc_ref, sh_ref, w_ref,
                     gq_ref, gkv_ref, wq_ref, wqs_ref, wk_ref, wv_ref, cos_ref, sin_ref,
                     mcw_ref, mcb_ref, mwq_ref, mwk_ref, gb_ref, hcw_ref, hcb_ref,
                     q_ref, k_ref, v_ref, mk_ref, mqt_ref, mvt_ref, pc_ref, pr_ref, og_ref,
                     x0_ref, z_ref, ext_scr, *, halo):
    tm = x_ref.shape[1]
    t = pl.program_id(1)
    x_ext = jnp.concatenate([xp_ref[0], x_ref[0], xn_ref[0]], axis=0)
    ms = jnp.mean(x_ext * x_ext, axis=-1, keepdims=True)
    h = x_ext * lax.rsqrt(ms + EPS) * g_ref[...] * (1.0 + sc_ref[0]) + sh_ref[0]
    row = lax.broadcasted_iota(jnp.int32, (tm + 2 * halo, 1), 0)
    outside = ((row < halo) & (t == 0)) | ((row >= tm + halo) & (t == pl.num_programs(1) - 1))
    hb = jnp.where(outside, 0.0, h).astype(BF16)
    for i in range(3):
        ext_scr[:, i * SEG:(i + 1) * SEG] = jnp.dot(hb, w_ref[:, i * SEG:(i + 1) * SEG],
                                                    preferred_element_type=F32)
    inner = pl.ds(halo, tm)

    cos = cos_ref[...]
    sin = sin_ref[...]
    scale = (MLA_NOPE + MLA_ROPE) ** -0.5 * math.log2(math.e)
    ql = ext_scr[inner, 0:MLA_Q_RANK]
    qn = ql * lax.rsqrt(jnp.mean(ql * ql, axis=-1, keepdims=True) + EPS) * gq_ref[...]
    qn = qn.astype(BF16)
    cos8 = jnp.concatenate([cos] * MLA_HEADS, axis=-1)
    sin8 = jnp.concatenate([sin] * MLA_HEADS, axis=-1)
    q = (jnp.dot(qn, wq_ref[...], preferred_element_type=F32) * cos8
         + jnp.dot(qn, wqs_ref[...], preferred_element_type=F32) * sin8) * scale
    q_ref[0] = q.astype(q_ref.dtype)
    kvl = ext_scr[inner, MLA_Q_RANK:MLA_Q_RANK + MLA_KV_RANK]
    kvn = kvl * lax.rsqrt(jnp.mean(kvl * kvl, axis=-1, keepdims=True) + EPS) * gkv_ref[...]
    kvb = kvn.astype(BF16)
    kn = jnp.dot(kvb, wk_ref[...], preferred_element_type=F32)
    v_ref[0] = jnp.dot(kvb, wv_ref[...], preferred_element_type=F32).astype(v_ref.dtype)
    blk = ext_scr[inner, SEG - LANES:SEG]
    kr = jnp.where(_lane_iota(blk.shape) < MLA_ROPE, blk, 0.0)
    kr = pltpu.roll(kr, ROPE_LANE, 1)
    kr = kr * cos + _swap_rope_halves(kr) * sin
    k_ref[0] = (kn + jnp.concatenate([kr] * MLA_HEADS, axis=-1)).astype(k_ref.dtype)

    n_chunk = tm // ML_CHUNK
    ml0 = SEG
    u = jax.nn.silu(_conv3(ext_scr.at[:, ml0:ml0 + ML_WIDTH], halo, tm, mcw_ref[...],
                           mcb_ref[...])).astype(BF16)
    mk_ref[0] = jnp.dot(u, mwk_ref[...], preferred_element_type=F32).astype(mk_ref.dtype)
    q_t = jnp.dot(u, mwq_ref[...], preferred_element_type=F32).T
    v_t = ext_scr[inner, ml0 + ML_WIDTH:ml0 + 2 * ML_WIDTH].T
    og_ref[0] = ext_scr[inner, ml0 + 2 * ML_WIDTH:ml0 + 3 * ML_WIDTH].astype(og_ref.dtype)
    xg = blk + gb_ref[...]
    lane = _lane_iota(xg.shape) - GATE_LANE
    is_gate = (lane >= 0) & (lane < 4 * ML_HEADS)
    is_forget = is_gate & ((lane // ML_HEADS) % 2 == 1)
    packed = jnp.where(is_forget, _log_sigmoid(xg), jnp.where(is_gate, xg, 0.0))
    lane_c = _lane_iota((ML_CHUNK, LANES)) - GATE_LANE
    fwd_lane = (lane_c >= ML_HEADS) & (lane_c < 2 * ML_HEADS)
    bwd_lane = (lane_c >= 3 * ML_HEADS) & (lane_c < 4 * ML_HEADS)
    r_idx = lax.broadcasted_iota(jnp.int32, (ML_CHUNK, ML_CHUNK), 0)
    c_idx = lax.broadcasted_iota(jnp.int32, (ML_CHUNK, ML_CHUNK), 1)
    lower = (c_idx <= r_idx).astype(BF16)
    for j in range(n_chunk):
        cols = slice(j * ML_CHUNK, (j + 1) * ML_CHUNK)
        part = packed[cols, :]
        prefix = None
        rest = part
        for _ in range(3):
            piece = rest.astype(BF16)
            term = jnp.dot(lower, piece, preferred_element_type=F32)
            prefix = term if prefix is None else prefix + term
            rest = rest - piece.astype(F32)
        suffix = prefix[ML_CHUNK - 1:ML_CHUNK, :] - prefix + part
        part = jnp.where(fwd_lane, prefix, jnp.where(bwd_lane, suffix, part))
        pc_ref[0, cols, :] = part
        pr_ref[0, j] = part.T
        mqt_ref[0, j] = q_t[:, cols].astype(mqt_ref.dtype)
        mvt_ref[0, j] = v_t[:, cols].astype(mvt_ref.dtype)

    hy0 = 2 * SEG
    uh = _conv3(ext_scr.at[:, hy0:hy0 + SEG], halo, tm, hcw_ref[...], hcb_ref[...])
    x0_ref[0] = uh[:, :HY_WIDTH].astype(x0_ref.dtype)
    z_ref[...] = (uh[:, HY_WIDTH:2 * HY_WIDTH] * uh[:, 2 * HY_WIDTH:]).astype(z_ref.dtype)


def _mixer_in(x, g, sc, sh, w_in, gq, gkv, wq, wqs, wk, wv, cos, sin, mcw, mcb, mwq, mwk, gate_b,
              hcw, hcb, tm):
    b, seq, d = x.shape
    halo = 8
    bm = sc.shape[0]
    mod_map = (lambda i, t: (i, 0, 0)) if bm > 1 else (lambda i, t: (0, 0, 0))
    prev, nxt = _halo_specs(seq, tm, halo, d)
    const = lambda i, t: (0, 0)
    whole = lambda a: pl.BlockSpec(a.shape, const, pipeline_mode=pl.Buffered(1))
    tok = lambda width: pl.BlockSpec((1, tm, width), lambda i, t: (i, t, 0))
    n_chunk = tm // ML_CHUNK
    chunked = lambda rows: pl.BlockSpec((1, n_chunk, rows, ML_CHUNK), lambda i, t: (i, t, 0, 0))
    hw = MLA_HEADS * HEAD_PAD
    vw = MLA_HEADS * MLA_V
    w = ML_WIDTH
    consts = (g, None, None, w_in, gq, gkv, wq, wqs, wk, wv)
    in_specs = [prev, tok(d), nxt]
    for a in consts:
        in_specs.append(pl.BlockSpec((1, 1, d), mod_map) if a is None else whole(a))
    in_specs += [pl.BlockSpec((tm, HEAD_PAD), lambda i, t: (t, 0))] * 2
    tail = (mcw, mcb, mwq, mwk, gate_b, hcw, hcb)
    in_specs += [whole(a) for a in tail]
    outs = pl.pallas_call(
        functools.partial(_mixer_in_kernel, halo=halo),
        grid=(b, seq // tm),
        in_specs=in_specs,
        out_specs=[
            tok(hw), tok(hw), tok(vw),
            tok(w), chunked(w), chunked(w), tok(LANES), chunked(LANES), tok(w),
            tok(HY_WIDTH),
            pl.BlockSpec((tm, HY_WIDTH), lambda i, t: (t, i)),
        ],
        out_shape=[
            jax.ShapeDtypeStruct((b, seq, hw), BF16),
            jax.ShapeDtypeStruct((b, seq, hw), BF16),
            jax.ShapeDtypeStruct((b, seq, vw), BF16),
            jax.ShapeDtypeStruct((b, seq, w), BF16),
            jax.ShapeDtypeStruct((b, seq // ML_CHUNK, w, ML_CHUNK), BF16),
            jax.ShapeDtypeStruct((b, seq // ML_CHUNK, w, ML_CHUNK), BF16),
            jax.ShapeDtypeStruct((b, seq, LANES), F32),
            jax.ShapeDtypeStruct((b, seq // ML_CHUNK, LANES, ML_CHUNK), F32),
            jax.ShapeDtypeStruct((b, seq, w), BF16),
            jax.ShapeDtypeStruct((b, seq, HY_WIDTH), BF16),
            jax.ShapeDtypeStruct((seq, b * HY_WIDTH), BF16),
        ],
        scratch_shapes=[pltpu.VMEM((tm + 2 * halo, 3 * SEG), F32)],
        compiler_params=_cparams(("parallel", "arbitrary")),
        name="mixer_in",
    )(x, x, x, g, sc, sh, w_in, gq, gkv, wq, wqs, wk, wv, cos, sin, *tail)
    return outs[0:3], outs[3:8], outs[8], outs[9:11]


def _ml_chunk(k, q_t, v_t, p_col, p_row, state_ref, m_ref, direction):
    t_len = k.shape[0]
    half = ML_HEAD_DIM
    s_idx = lax.broadcasted_iota(jnp.int32, (t_len, t_len), 0)
    t_idx = lax.broadcasted_iota(jnp.int32, (t_len, t_len), 1)
    keep = (s_idx <= t_idx) if direction == 0 else (s_idx >= t_idx)
    row_p = lax.broadcasted_iota(jnp.int32, (2 * half, t_len), 0)
    last = t_len - 1 if direction == 0 else 0
    outs = []
    for h in range(ML_HEADS):
        pair, odd = divmod(h, 2)
        sl = slice(pair * 2 * half, (pair + 1) * 2 * half)
        mine = (row_p >= half) if odd else (row_p < half)
        one_row = 0 if odd else half
        li_l = GATE_LANE + 2 * direction * ML_HEADS + h
        cu_l = li_l + ML_HEADS
        b_c = p_col[:, li_l:li_l + 1] - p_col[:, cu_l:cu_l + 1]
        li_r = p_row[li_l:li_l + 1, :]
        cu_r = p_row[cu_l:cu_l + 1, :]
        m = m_ref[h]
        state = state_ref[h]

        b_m = jnp.where(keep, b_c, -jnp.inf)
        m_out = cu_r + jnp.maximum(m, jnp.max(b_m, axis=0, keepdims=True))
        kh = k[:, sl]
        qh_t = jnp.where(mine, q_t[sl, :], 0).astype(BF16)
        vh_t = jnp.where(mine, v_t[sl, :].astype(F32), (row_p == one_row).astype(F32))
        s_t = jnp.dot(kh, qh_t, preferred_element_type=F32) * jnp.exp(b_m + (cu_r - m_out))
        a = jnp.exp(cu_r + m - m_out)
        lhs = jnp.concatenate([vh_t.astype(BF16), state.astype(BF16)], axis=1)
        rhs = jnp.concatenate([s_t.astype(BF16), (qh_t.astype(F32) * a).astype(BF16)], axis=0)
        tot = jnp.dot(lhs, rhs, preferred_element_type=F32)
        den = jnp.maximum(jnp.abs(tot[one_row:one_row + 1, :]), jnp.exp(-m_out))
        outs.append(tot * (1.0 / den))

        cum_last = cu_r[:, last:last + 1]
        g = cum_last - cu_r + li_r
        m_new = jnp.maximum(cum_last + m, jnp.max(g, axis=1, keepdims=True))
        wts = jnp.exp(g - m_new)
        decay = jnp.exp(cum_last + m - m_new)
        upd = jnp.dot((vh_t * wts).astype(BF16), kh, preferred_element_type=F32)
        state_ref[h] = decay * state + upd
        m_ref[h] = m_new
    pairs = []
    for pair in range(ML_HEADS // 2):
        pairs.append(jnp.where(row_p < half, outs[2 * pair], outs[2 * pair + 1]))
    return jnp.concatenate(pairs, axis=0)


def _ml_scan_kernel(kc_ref, qtc_ref, vtc_ref, pcc_ref, prc_ref, oc_ref,
                    kx_ref, qtx_ref, vtx_ref, pcx_ref, prx_ref, ox_ref, ng_ref, bd_ref,
                    hx_ref, hc_ref, fx_scr, bx_scr, fc_scr, bc_scr, state_scr, m_scr):
    t_len = ML_CHUNK
    n_c = kc_ref.shape[1] // t_len
    n_x = kx_ref.shape[1] // t_len
    state_scr[...] = jnp.zeros_like(state_scr)
    m_scr[...] = jnp.zeros_like(m_scr)

    def rows_of(i):
        return pl.ds(pl.multiple_of(i * t_len, t_len), t_len)

    def scan(refs, scrs, n):
        k_ref, qt_ref, vt_ref, pc_ref, pr_ref = refs

        def body(j, _):
            for direction, i in ((0, j), (1, n - 1 - j)):
                rows = rows_of(i)
                scrs[direction][i] = _ml_chunk(
                    k_ref[0, rows, :], qt_ref[0, i], vt_ref[0, i], pc_ref[0, rows, :],
                    pr_ref[0, i], state_scr.at[direction], m_scr.at[direction], direction)
            return 0

        lax.fori_loop(0, n, body, 0, unroll=2)

    def finish(scrs, o_ref, out_ref, n):
        def body(i, _):
            rows = rows_of(i)
            h = (scrs[0][i] + scrs[1][i]).T
            y = h * jax.nn.sigmoid(o_ref[0, rows, :].astype(F32))
            ms = _dot_split3(y * y, bd_ref[...])
            out_ref[0, rows, :] = (y * lax.rsqrt(ms + EPS) * ng_ref[...]).astype(out_ref.dtype)
            return 0

        lax.fori_loop(0, n, body, 0, unroll=2)

    scan((kc_ref, qtc_ref, vtc_ref, pcc_ref, prc_ref), (fc_scr, bc_scr), n_c)
    scan((kx_ref, qtx_ref, vtx_ref, pcx_ref, prx_ref), (fx_scr, bx_scr), n_x)
    finish((fc_scr, bc_scr), oc_ref, hc_ref, n_c)
    finish((fx_scr, bx_scr), ox_ref, hx_ref, n_x)


def _ml_scan(feat_c, gate_c, feat_x, gate_x, norm_g, bd):
    b, lc, w = feat_c[0].shape
    lx = feat_x[0].shape[1]

    def seq_specs(n):
        n_chunk = n // ML_CHUNK
        return [
            pl.BlockSpec((1, n, w), lambda i: (i, 0, 0)),
            pl.BlockSpec((1, n_chunk, w, ML_CHUNK), lambda i: (i, 0, 0, 0)),
            pl.BlockSpec((1, n_chunk, w, ML_CHUNK), lambda i: (i, 0, 0, 0)),
            pl.BlockSpec((1, n, LANES), lambda i: (i, 0, 0)),
            pl.BlockSpec((1, n_chunk, LANES, ML_CHUNK), lambda i: (i, 0, 0, 0)),
            pl.BlockSpec((1, n, w), lambda i: (i, 0, 0)),
        ]

    const = lambda i: (0, 0)
    return pl.pallas_call(
        _ml_scan_kernel,
        grid=(b,),
        in_specs=seq_specs(lc) + seq_specs(lx) + [
            pl.BlockSpec((1, w), const),
            pl.BlockSpec((w, w), const),
        ],
        out_specs=[
            pl.BlockSpec((1, lx, w), lambda i: (i, 0, 0)),
            pl.BlockSpec((1, lc, w), lambda i: (i, 0, 0)),
        ],
        out_shape=[
            jax.ShapeDtypeStruct((b, lx, w), BF16),
            jax.ShapeDtypeStruct((b, lc, w), BF16),
        ],
        scratch_shapes=[
            pltpu.VMEM((lx // ML_CHUNK, w, ML_CHUNK), F32),
            pltpu.VMEM((lx // ML_CHUNK, w, ML_CHUNK), F32),
            pltpu.VMEM((lc // ML_CHUNK, w, ML_CHUNK), F32),
            pltpu.VMEM((lc // ML_CHUNK, w, ML_CHUNK), F32),
            pltpu.VMEM((2, ML_HEADS, 2 * ML_HEAD_DIM, 2 * ML_HEAD_DIM), F32),
            pltpu.VMEM((2, ML_HEADS, 1, 1), F32),
        ],
        compiler_params=_cparams(("parallel",)),
        name="ml_scan",
    )(*feat_c, gate_c, *feat_x, gate_x, norm_g, bd)


@functools.lru_cache(maxsize=None)
def _dft_mats(seq, kt):
    n_fft = 2 * seq
    k = np.arange(seq, dtype=np.int64)[:, None]
    n = np.arange(seq, dtype=np.int64)[None, :]
    ang = (2.0 * np.pi / n_fft) * ((k * n) % n_fft).astype(np.float64)
    cos = np.cos(ang)
    msin = -np.sin(ang)
    msin[0, :] = 1.0 - 2.0 * (np.arange(seq) % 2)
    fwd = np.stack([cos.reshape(seq // kt, kt, seq), msin.reshape(seq // kt, kt, seq)], axis=1)
    fwd = fwd.reshape(2 * seq, seq)
    wgt = np.full((seq, 1), 2.0 / n_fft)
    wgt[0, 0] = 1.0 / n_fft
    inv = np.stack([(cos * wgt).reshape(seq // kt, kt, seq),
                    (msin * wgt).reshape(seq // kt, kt, seq)], axis=1)
    inv = inv.reshape(2 * seq, seq).T
    return jnp.asarray(fwd, dtype=BF16), jnp.asarray(np.ascontiguousarray(inv), dtype=BF16)


def _dft_fwd_kernel(f_ref, z_ref, *rest, kt, reps):
    zz = jnp.dot(f_ref[...], z_ref[...], preferred_element_type=F32)
    if not rest[1:]:
        rest[0][...] = zz
        return
    h_ref, l1_ref, y_ref = rest
    w = HY_WIDTH
    inv_l1 = 1.0 / l1_ref[...]
    hc, hs = h_ref[:kt, :], h_ref[kt:, :]
    first = (lax.broadcasted_iota(jnp.int32, (kt, w), 0) == 0) & (pl.program_id(0) == 0)
    hre = (hc[:, :w] + hc[:, w:]) * inv_l1
    him = jnp.where(first, 0.0, (hs[:, :w] - hs[:, w:]) * inv_l1)
    hre2 = jnp.where(first, (hs[:, :w] + hs[:, w:]) * inv_l1, hre)
    tile = lambda r: jnp.concatenate([r] * reps, axis=-1)
    hre, him, hre2 = tile(hre), tile(him), tile(hre2)
    zre, zim = zz[:kt], zz[kt:]
    y_ref[:kt, :] = (zre * hre - zim * him).astype(y_ref.dtype)
    y_ref[kt:, :] = (zre * him + zim * hre2).astype(y_ref.dtype)


def _dft_fwd(fwd, z, spectrum, kt, cb):
    seq = z.shape[0]
    ncol = z.shape[1]
    in_specs = [
        pl.BlockSpec((2 * kt, seq), lambda i, j: (i, 0)),
        pl.BlockSpec((seq, cb), lambda i, j: (0, j)),
    ]
    args = [fwd, z]
    out_dtype = F32
    if spectrum is not None:
        in_specs += [pl.BlockSpec((2 * kt, 2 * HY_WIDTH), lambda i, j: (i, 0)),
                     pl.BlockSpec((1, HY_WIDTH), lambda i, j: (0, 0))]
        args += list(spectrum)
        out_dtype = BF16
    return pl.pallas_call(
        functools.partial(_dft_fwd_kernel, kt=kt, reps=cb // HY_WIDTH),
        grid=(seq // kt, ncol // cb),
        in_specs=in_specs,
        out_specs=pl.BlockSpec((2 * kt, cb), lambda i, j: (i, j)),
        out_shape=jax.ShapeDtypeStruct((2 * seq, ncol), out_dtype),
        compiler_params=_cparams(("parallel", "arbitrary")),
        name="dft_fwd",
    )(*args)


def _dft_inv_kernel(g_ref, y_ref, z_ref, x0_ref, bias_ref, o_ref, *, nb):
    y = jnp.dot(g_ref[...], y_ref[...], preferred_element_type=F32)
    bias = bias_ref[...]
    for bb in range(nb):
        cols = slice(bb * HY_WIDTH, (bb + 1) * HY_WIDTH)
        yb = y[:, cols] + z_ref[:, cols].astype(F32) * bias
        o_ref[bb] = (x0_ref[bb].astype(F32) * yb).astype(o_ref.dtype)


def _dft_inv(inv, y, z, x0, bias, tm, nb):
    b, seq, _ = x0.shape
    cb = nb * HY_WIDTH
    return pl.pallas_call(
        functools.partial(_dft_inv_kernel, nb=nb),
        grid=(seq // tm, b // nb),
        in_specs=[
            pl.BlockSpec((tm, 2 * seq), lambda t, j: (t, 0)),
            pl.BlockSpec((2 * seq, cb), lambda t, j: (0, j)),
            pl.BlockSpec((tm, cb), lambda t, j: (t, j)),
            pl.BlockSpec((nb, tm, HY_WIDTH), lambda t, j: (j, t, 0)),
            pl.BlockSpec((1, HY_WIDTH), lambda t, j: (0, 0)),
        ],
        out_specs=pl.BlockSpec((nb, tm, HY_WIDTH), lambda t, j: (j, t, 0)),
        out_shape=jax.ShapeDtypeStruct((b, seq, HY_WIDTH), BF16),
        compiler_params=_cparams(("parallel", "arbitrary")),
        name="dft_inv",
    )(inv, y, z, x0, bias)


@functools.lru_cache(maxsize=None)
def _filter_features(seq):
    t = np.linspace(0.0, 1.0, seq)[:, None]
    omega = 2.0 * math.pi * np.arange(seq) / seq
    bands = np.linspace(1e-4, HY_BANDS - 1, HY_BANDS)
    ang = omega[:, None] * bands[None, :]
    feats = np.zeros((seq, LANES), np.float32)
    feats[:, :1 + 2 * HY_BANDS] = np.concatenate([t, np.cos(ang), -np.sin(ang)], axis=-1)
    deltas = np.abs(np.linspace(math.log(HY_DECAY_TARGET) / HY_SLOW_PCT,
                                math.log(HY_DECAY_TARGET) / HY_FAST_PCT, HY_WIDTH))
    return jnp.asarray(feats), jnp.asarray(deltas[None, :].astype(np.float32))


def _filter_kernel(z_ref, dl_ref, w1_ref, b1_ref, w2_ref, b2_ref, w3_ref, fr_ref, taps_ref,
                   l1_ref):
    t = pl.program_id(0)
    z = z_ref[...]
    fr = fr_ref[...]
    dot = functools.partial(jnp.dot, precision=HIGHEST, preferred_element_type=F32)
    hdn = jnp.sin(fr * (dot(z, w1_ref[...]) + b1_ref[...]))
    hdn = jnp.sin(fr * (dot(hdn, w2_ref[...]) + b2_ref[...]))
    filt = dot(hdn, w3_ref[...])
    window = jnp.exp(-z[:, 0:1] * dl_ref[...]) + HY_SHIFT
    h_fwd = filt[:, :HY_WIDTH] * window
    h_bwd = filt[:, HY_WIDTH:] * window
    first = (lax.broadcasted_iota(jnp.int32, h_bwd.shape, 0) == 0) & (t == 0)
    h_bwd = jnp.where(first, 0.0, h_bwd)
    taps_ref[...] = jnp.concatenate([h_fwd, h_bwd], axis=-1).astype(taps_ref.dtype)
    part = jnp.sum(jnp.abs(h_fwd) + jnp.abs(h_bwd), axis=0, keepdims=True)

    @pl.when(t == 0)
    def _():
        l1_ref[...] = part

    @pl.when(t > 0)
    def _():
        l1_ref[...] += part


def _hyena_filter(seq, tm, w1, b1, w2, b2, w3, sin_freq):
    feats, deltas = _filter_features(seq)
    hid = w2.shape[0]
    w1p = jnp.zeros((LANES, hid), F32).at[:w1.shape[0]].set(w1)
    const = lambda t: (0, 0)
    row = lambda v: v.reshape(1, -1)
    return pl.pallas_call(
        _filter_kernel,
        grid=(seq // tm,),
        in_specs=[
            pl.BlockSpec((tm, LANES), lambda t: (t, 0)),
            pl.BlockSpec((1, HY_WIDTH), const),
            pl.BlockSpec((LANES, hid), const),
            pl.BlockSpec((1, hid), const),
            pl.BlockSpec((hid, hid), const),
            pl.BlockSpec((1, hid), const),
            pl.BlockSpec((hid, 2 * HY_WIDTH), const),
            pl.BlockSpec((1, hid), const),
        ],
        out_specs=[
            pl.BlockSpec((tm, 2 * HY_WIDTH), lambda t: (t, 0)),
            pl.BlockSpec((1, HY_WIDTH), const),
        ],
        out_shape=[
            jax.ShapeDtypeStruct((seq, 2 * HY_WIDTH), BF16),
            jax.ShapeDtypeStruct((1, HY_WIDTH), F32),
        ],
        compiler_params=_cparams(("arbitrary",)),
        name="hy_filter",
    )(feats, deltas, w1p, row(b1), w2, row(b2), w3, row(sin_freq))


FFT_RADIX = 16


@functools.lru_cache(maxsize=None)
def _fft_consts(seq):
    n_fft = 2 * seq
    s_half = FFT_RADIX // 2
    f_len = n_fft // FFT_RADIX
    ks = np.arange(s_half + 1)
    theta = 2.0 * np.pi * np.outer(ks, np.arange(s_half)) / FFT_RADIX
    wgt = np.where((ks == 0) | (ks == s_half), 1.0, 2.0)[:, None] / n_fft
    coef = np.concatenate([np.cos(theta), -np.sin(theta), wgt * np.cos(theta),
                           -wgt * np.sin(theta)], axis=1).astype(np.float32)
    f = np.arange(f_len, dtype=np.int64)
    fwd, inv = [], []
    for k0 in ks:
        k = k0 + FFT_RADIX * np.arange(f_len, dtype=np.int64)
        ang = (2.0 * np.pi / n_fft) * ((k[:, None] * f[None, :]) % n_fft)
        m_re, m_im = np.cos(ang), -np.sin(ang)
        fwd.append(np.concatenate([m_re, m_im], axis=0))
        inv.append(np.concatenate([m_re.T, m_im.T], axis=0))
    return (jnp.asarray(coef), jnp.asarray(np.stack(fwd), dtype=BF16),
            jnp.asarray(np.stack(inv), dtype=BF16))


def _fft_forward(coef_ref, ks, z_ref, fm_ref, f_len, width):
    s_half = FFT_RADIX // 2
    a_re = a_im = None
    for s in range(s_half):
        blk = z_ref[s * f_len:(s + 1) * f_len, :].astype(F32)
        t_re, t_im = coef_ref[ks, s] * blk, coef_ref[ks, s_half + s] * blk
        a_re = t_re if a_re is None else a_re + t_re
        a_im = t_im if a_im is None else a_im + t_im
    p = jnp.dot(fm_ref[0], jnp.concatenate([a_re, a_im], axis=1).astype(BF16),
                preferred_element_type=F32)
    x_re = p[:f_len, :width] - p[f_len:, width:]
    x_im = p[f_len:, :width] + p[:f_len, width:]
    return x_re, x_im


def _fft_filter_kernel(coef_ref, taps_ref, fm_ref, o_ref, *, f_len):
    x_re, x_im = _fft_forward(coef_ref, pl.program_id(0), taps_ref, fm_ref, f_len,
                              taps_ref.shape[1])
    o_ref[0, :f_len, :] = x_re
    o_ref[0, f_len:, :] = x_im


def _fft_conv_kernel(coef_ref, z_ref, fm_ref, im_ref, h_ref, l1_ref, x0_ref, bias_ref, o_ref,
                     y_scr, zf_scr, *, f_len):
    ks = pl.program_id(1)
    s_half = FFT_RADIX // 2
    w = HY_WIDTH

    @pl.when(ks == 0)
    def _():
        zf = z_ref[...].astype(F32)
        zf_scr[...] = zf
        y_scr[...] = zf * bias_ref[...]

    x_re, x_im = _fft_forward(coef_ref, ks, zf_scr, fm_ref, f_len, w)
    inv_l1 = 1.0 / l1_ref[...]
    h_re = (h_ref[0, :f_len, :w] + h_ref[0, :f_len, w:]) * inv_l1
    h_im = (h_ref[0, f_len:, :w] - h_ref[0, f_len:, w:]) * inv_l1
    y_re = x_re * h_re - x_im * h_im
    y_im = x_re * h_im + x_im * h_re
    r = jnp.dot(im_ref[0], jnp.concatenate([y_re, y_im], axis=1).astype(BF16),
                preferred_element_type=F32)
    b_re = r[:f_len, :w] + r[f_len:, w:]
    b_im = r[:f_len, w:] - r[f_len:, :w]

    for s in range(s_half):
        rows = slice(s * f_len, (s + 1) * f_len)
        y_scr[rows, :] += (coef_ref[ks, 2 * s_half + s] * b_re
                           + coef_ref[ks, 3 * s_half + s] * b_im)

    @pl.when(ks == s_half)
    def _():
        o_ref[0] = (x0_ref[0].astype(F32) * y_scr[...]).astype(o_ref.dtype)


def _hyena_fft(z, x0, taps, l1, bias_d):
    b, seq, w = x0.shape
    coef, fwd_m, inv_m = _fft_consts(seq)
    n_ks = FFT_RADIX // 2 + 1
    f_len = 2 * seq // FFT_RADIX
    smem = pl.BlockSpec(memory_space=pltpu.SMEM)
    h_raw = pl.pallas_call(
        functools.partial(_fft_filter_kernel, f_len=f_len),
        grid=(n_ks,),
        in_specs=[smem,
                  pl.BlockSpec((seq, 2 * w), lambda k: (0, 0)),
                  pl.BlockSpec((1, 2 * f_len, f_len), lambda k: (k, 0, 0))],
        out_specs=pl.BlockSpec((1, 2 * f_len, 2 * w), lambda k: (k, 0, 0)),
        out_shape=jax.ShapeDtypeStruct((n_ks, 2 * f_len, 2 * w), F32),
        compiler_params=_cparams(("parallel",)),
        name="hy_fft_filter",
    )(coef, taps, fwd_m)
    return pl.pallas_call(
        functools.partial(_fft_conv_kernel, f_len=f_len),
        grid=(b, n_ks),
        in_specs=[smem,
                  pl.BlockSpec((seq, w), lambda i, k: (0, i)),
                  pl.BlockSpec((1, 2 * f_len, f_len), lambda i, k: (k, 0, 0)),
                  pl.BlockSpec((1, 2 * f_len, f_len), lambda i, k: (k, 0, 0)),
                  pl.BlockSpec((1, 2 * f_len, 2 * w), lambda i, k: (k, 0, 0)),
                  pl.BlockSpec((1, w), lambda i, k: (0, 0)),
                  pl.BlockSpec((1, seq, w), lambda i, k: (i, 0, 0)),
                  pl.BlockSpec((1, w), lambda i, k: (0, 0))],
        out_specs=pl.BlockSpec((1, seq, w), lambda i, k: (i, 0, 0)),
        out_shape=jax.ShapeDtypeStruct((b, seq, w), BF16),
        scratch_shapes=[pltpu.VMEM((seq, w), F32), pltpu.VMEM((seq, w), F32)],
        compiler_params=_cparams(("parallel", "arbitrary")),
        name="hy_fft_conv",
    )(coef, z, fwd_m, inv_m, h_raw, l1, x0, bias_d)


def _hyena(x0, z, filt_w, bias_d, tm, kt, nb):
    seq = x0.shape[1]
    taps, l1 = _hyena_filter(seq, tm, *filt_w)
    if 2 * seq // FFT_RADIX >= LANES:
        return _hyena_fft(z, x0, taps, l1, bias_d)
    fwd, inv = _dft_mats(seq, kt)
    raw = _dft_fwd(fwd, taps, None, kt, 2 * HY_WIDTH)
    y = _dft_fwd(fwd, z, (raw, l1), kt, nb * HY_WIDTH)
    return _dft_inv(inv, y, z, x0, bias_d, tm, nb)


def _ffn_kernel(xp_ref, x_ref, xn_ref, ap_ref, a_ref, an_ref, mp_ref, m_ref, mn_ref, hp_ref,
                h_ref, hn_ref, wo_ref, g1_ref, g_ref, sc_ref, sh_ref, gate_ref, wup_ref, cw_ref,
                cb_ref, wdn_ref, fg_ref, o_ref, xe_scr, u_scr, acc_scr, *, halo, mix_halo,
                final_norm):
    tm = x_ref.shape[1]
    d_ff = wdn_ref.shape[0]
    n_chunks = d_ff // FF_CHUNK
    t = pl.program_id(1)

    def with_halo(p_ref, c_ref, n_ref):
        prev = p_ref[0].astype(F32)[mix_halo - halo:, :]
        nxt = n_ref[0].astype(F32)[:halo, :]
        return jnp.concatenate([prev, c_ref[0].astype(F32), nxt], axis=0).astype(BF16)

    lo = 0
    mixed = None
    for refs in ((ap_ref, a_ref, an_ref), (mp_ref, m_ref, mn_ref), (hp_ref, h_ref, hn_ref)):
        width = refs[1].shape[2]
        part = jnp.dot(with_halo(*refs), wo_ref[lo:lo + width, :], preferred_element_type=F32)
        mixed = part if mixed is None else mixed + part
        lo += width
    x_ext = jnp.concatenate([xp_ref[0], x_ref[0], xn_ref[0]], axis=0) + g1_ref[0] * mixed

    def cols(c):
        return (slice(c * FF_CHUNK, (c + 1) * FF_CHUNK),
                slice(d_ff + c * FF_CHUNK, d_ff + (c + 1) * FF_CHUNK))

    def norm_mod(x):
        ms = jnp.mean(x * x, axis=-1, keepdims=True)
        return x * lax.rsqrt(ms + EPS) * g_ref[...] * (1.0 + sc_ref[0]) + sh_ref[0]

    row = lax.broadcasted_iota(jnp.int32, (tm + 2 * halo, 1), 0)
    outside = ((row < halo) & (t == 0)) | ((row >= tm + halo) & (t == pl.num_programs(1) - 1))
    xe_scr[...] = jnp.where(outside, 0.0, norm_mod(x_ext)).astype(BF16)
    x_mid = x_ext[halo:halo + tm, :]

    def up_proj(c, slot):
        n_rows = tm + 2 * halo
        split = n_rows // 32 * 16
        for rows in (slice(0, split), slice(split, n_rows)):
            xe_b = xe_scr[rows, :]
            for half, sl in enumerate(cols(c)):
                u_scr[slot, rows, half * FF_CHUNK:(half + 1) * FF_CHUNK] = jnp.dot(
                    xe_b, wup_ref[:, sl], preferred_element_type=F32)

    up_proj(0, 0)
    for c in range(n_chunks):
        slot = c % 2
        if c + 1 < n_chunks:
            up_proj(c + 1, 1 - slot)
        gate_cols, val_cols = cols(c)
        cw = jnp.concatenate([cw_ref[:, gate_cols], cw_ref[:, val_cols]], axis=-1)
        cb = jnp.concatenate([cb_ref[:, gate_cols], cb_ref[:, val_cols]], axis=-1)
        uc = _conv3(u_scr.at[slot], halo, tm, cw, cb)
        act = jax.nn.silu(uc[:, :FF_CHUNK]) * uc[:, FF_CHUNK:]
        acc_scr[:, gate_cols] = act.astype(BF16)
    down = jnp.dot(acc_scr[...], wdn_ref[...], preferred_element_type=F32)
    y = x_mid + gate_ref[0] * down
    if final_norm:
        y = y * lax.rsqrt(jnp.mean(y * y, axis=-1, keepdims=True) + EPS) * fg_ref[...]
    o_ref[0] = y


def _out_ffn(x, att, ml, hy, w_out, gate1, g, sc, sh, gate, wup, cw, cb, wdn, final_g, tm,
             final_norm):
    b, seq, d = x.shape
    halo = 8
    mix_halo = 16
    bm = sc.shape[0]
    mod_map = (lambda i, t: (i, 0, 0)) if bm > 1 else (lambda i, t: (0, 0, 0))
    prev, nxt = _halo_specs(seq, tm, halo, d)
    const2 = lambda i, t: (0, 0)
    d_ff = wdn.shape[0]
    mix_specs, mix_args = [], []
    for a in (att, ml, hy):
        width = a.shape[2]
        m_prev, m_nxt = _halo_specs(seq, tm, mix_halo, width)
        mix_specs += [m_prev, pl.BlockSpec((1, tm, width), lambda i, t: (i, t, 0)), m_nxt]
        mix_args += [a, a, a]
    return pl.pallas_call(
        functools.partial(_ffn_kernel, halo=halo, mix_halo=mix_halo, final_norm=final_norm),
        grid=(b, seq // tm),
        in_specs=[
            prev,
            pl.BlockSpec((1, tm, d), lambda i, t: (i, t, 0)),
            nxt,
            *mix_specs,
            pl.BlockSpec((d, d), const2, pipeline_mode=pl.Buffered(1)),
            pl.BlockSpec((1, 1, d), mod_map),
            pl.BlockSpec((1, d), const2),
            pl.BlockSpec((1, 1, d), mod_map),
            pl.BlockSpec((1, 1, d), mod_map),
            pl.BlockSpec((1, 1, d), mod_map),
            pl.BlockSpec((d, 2 * d_ff), const2, pipeline_mode=pl.Buffered(1)),
            pl.BlockSpec((3, 2 * d_ff), const2),
            pl.BlockSpec((1, 2 * d_ff), const2),
            pl.BlockSpec((d_ff, d), const2, pipeline_mode=pl.Buffered(1)),
            pl.BlockSpec((1, d), const2),
        ],
        out_specs=pl.BlockSpec((1, tm, d), lambda i, t: (i, t, 0)),
        out_shape=jax.ShapeDtypeStruct((b, seq, d), F32),
        scratch_shapes=[
            pltpu.VMEM((tm + 2 * halo, d), BF16),
            pltpu.VMEM((2, tm + 2 * halo, 2 * FF_CHUNK), F32),
            pltpu.VMEM((tm, d_ff), BF16),
        ],
        compiler_params=_cparams(("parallel", "arbitrary")),
        name="out_ffn",
    )(x, x, x, *mix_args, w_out, gate1, g, sc, sh, gate, wup, cw, cb, wdn, final_g)


def _layout_w_in(w):
    d = w.shape[0]
    ml_lo = N_MLA_IN
    hy_lo = N_MLA_IN + N_ML_IN
    gates = w[:, ml_lo + 3 * ML_WIDTH:hy_lo]
    pad = jnp.zeros((d, LANES - MLA_ROPE - 4 * ML_HEADS), w.dtype)
    return jnp.concatenate([w[:, :N_MLA_IN], gates, pad,
                            w[:, ml_lo:ml_lo + 3 * ML_WIDTH], w[:, hy_lo:]], axis=1).astype(BF16)


def _layout_w_uq(w):
    r = w.shape[0]
    w = w.reshape(r, MLA_HEADS, MLA_NOPE + MLA_ROPE)
    pad = jnp.zeros((r, MLA_HEADS, HEAD_PAD - MLA_NOPE - MLA_ROPE), w.dtype)
    return jnp.concatenate([w, pad], axis=-1).reshape(r, MLA_HEADS * HEAD_PAD).astype(BF16)


def _layout_w_uq_swapped(w):
    r = w.shape[0]
    w = w.reshape(r, MLA_HEADS, MLA_NOPE + MLA_ROPE)
    half = MLA_ROPE // 2
    lo = jnp.zeros((r, MLA_HEADS, ROPE_LANE), w.dtype)
    hi = jnp.zeros((r, MLA_HEADS, HEAD_PAD - ROPE_LANE - MLA_ROPE), w.dtype)
    out = jnp.concatenate([lo, w[..., MLA_NOPE + half:], w[..., MLA_NOPE:MLA_NOPE + half], hi],
                          axis=-1)
    return out.reshape(r, MLA_HEADS * HEAD_PAD).astype(BF16)


def _layout_w_ukv(w):
    r = w.shape[0]
    w = w.reshape(r, MLA_HEADS, MLA_NOPE + MLA_V)
    pad = jnp.zeros((r, MLA_HEADS, HEAD_PAD - MLA_NOPE), w.dtype)
    wk = jnp.concatenate([w[..., :MLA_NOPE], pad], axis=-1).reshape(r, MLA_HEADS * HEAD_PAD)
    wv = w[..., MLA_NOPE:].reshape(r, MLA_HEADS * MLA_V)
    return wk.astype(BF16), wv.astype(BF16)


def _block_diag(w):
    h, d, _ = w.shape
    eye = jnp.eye(h, dtype=w.dtype)
    return jnp.einsum('hde,hg->hdge', w, eye).reshape(h * d, h * d)


@functools.lru_cache(maxsize=None)
def _rope_tables(n_rows, rope):
    n_tok = n_rows * GRID_W
    cos_t = np.zeros((n_tok, HEAD_PAD), np.float32)
    sin_t = np.zeros((n_tok, HEAD_PAD), np.float32)
    cos_t[:, :ROPE_LANE + MLA_ROPE] = 1.0
    if rope:
        n_freq = MLA_ROPE // 4
        half = MLA_ROPE // 2
        inv = ROPE_THETA ** (-np.arange(n_freq, dtype=np.float64) / n_freq)
        row = np.repeat(np.arange(n_rows, dtype=np.float64), GRID_W)
        col = np.tile(np.arange(GRID_W, dtype=np.float64), n_rows)
        ang = np.concatenate([row[:, None] * inv, col[:, None] * inv], axis=-1)
        cos_t[:, ROPE_LANE:ROPE_LANE + half] = np.cos(ang)
        cos_t[:, ROPE_LANE + half:ROPE_LANE + MLA_ROPE] = np.cos(ang)
        sin_t[:, ROPE_LANE:ROPE_LANE + half] = -np.sin(ang)
        sin_t[:, ROPE_LANE + half:ROPE_LANE + MLA_ROPE] = np.sin(ang)
    return jnp.asarray(cos_t), jnp.asarray(sin_t)


def _ada_kernel(c_ref, w_ref, b_ref, o_ref):
    act = jax.nn.silu(c_ref[...])
    o_ref[0] = jnp.dot(act, w_ref[0], precision=HIGHEST, preferred_element_type=F32) + b_ref[0]


def _ada_mod(cvec, ada_w, ada_b):
    depth, d, n = ada_w.shape
    rows = cvec.shape[0]
    tn = n // 4
    return pl.pallas_call(
        _ada_kernel,
        grid=(depth, n // tn),
        in_specs=[
            pl.BlockSpec((rows, d), lambda l, j: (0, 0)),
            pl.BlockSpec((1, d, tn), lambda l, j: (l, 0, j)),
            pl.BlockSpec((1, 1, tn), lambda l, j: (l, 0, j)),
        ],
        out_specs=pl.BlockSpec((1, rows, tn), lambda l, j: (l, 0, j)),
        out_shape=jax.ShapeDtypeStruct((depth, rows, n), F32),
        compiler_params=_cparams(("parallel", "parallel")),
        name="ada_mod",
    )(cvec, ada_w, ada_b.reshape(depth, 1, n))


def kernel(x, c, ctx, c_ctx, ada_w, ada_b, norm1_g, norm2_g, w_in, mla_q_norm_g, mla_kv_norm_g,
           mla_w_uq, mla_w_ukv, ml_conv_w, ml_conv_b, ml_wq, ml_wk, ml_gate_b, ml_norm_g,
           hy_conv_w, hy_conv_b, hy_w1, hy_b1, hy_w2, hy_b2, hy_w3, hy_sin_freq, hy_bias_d,
           w_out, ffn_w_up, ffn_conv_w, ffn_conv_b, ffn_w_down, final_norm_g):
    b, seq, d = x.shape
    lc = ctx.shape[1]
    depth = ada_w.shape[0]
    tmx = min(512, seq)
    tmc = min(256, lc)
    cos_x, sin_x = _rope_tables(seq // GRID_W, True)
    cos_c, sin_c = _rope_tables(lc // GRID_W, False)
    head_mean = _block_diag(
        jnp.full((ML_HEADS, ML_HEAD_DIM, ML_HEAD_DIM), 1.0 / ML_HEAD_DIM, F32)).astype(BF16)
    row = lambda v: v.reshape(1, -1)

    n_cond = -(-(b + 1) // 8) * 8
    cvec = jnp.concatenate([c, c_ctx[None], jnp.zeros((n_cond - b - 1, d), c.dtype)], axis=0)
    mod = _ada_mod(cvec, ada_w, ada_b)

    for i in range(depth):
        last = i == depth - 1
        mx = [m[:, None, :] for m in jnp.split(mod[i, :b], 6, axis=-1)]
        mc = [m[:, None, :] for m in jnp.split(mod[i, b:b + 1], 6, axis=-1)]

        w_in_l = _layout_w_in(w_in[i])
        wq = _layout_w_uq(mla_w_uq[i])
        wk, wv = _layout_w_ukv(mla_w_ukv[i])
        wq_bd = _block_diag(ml_wq[i]).astype(BF16)
        wk_bd = (_block_diag(ml_wk[i]) * (ML_HEAD_DIM ** -0.5)).astype(BF16)
        gate_b = jnp.zeros((1, LANES), F32).at[0, GATE_LANE:GATE_LANE + 4 * ML_HEADS].set(
            ml_gate_b[i].reshape(-1))
        wup, wdn = ffn_w_up[i].astype(BF16), ffn_w_down[i].astype(BF16)
        fcw, fcb = ffn_conv_w[i], row(ffn_conv_b[i])
        w_out_l = w_out[i].astype(BF16)
        filt_w = (hy_w1[i], hy_b1[i], hy_w2[i], hy_b2[i], hy_w3[i], hy_sin_freq[i])
        g1 = row(norm1_g[i])

        gq, gkv = row(mla_q_norm_g[i]), row(mla_kv_norm_g[i])
        wqs = _layout_w_uq_swapped(mla_w_uq[i])
        mcw, mcb = ml_conv_w[i], row(ml_conv_b[i])
        hcw, hcb, hbias = hy_conv_w[i], row(hy_conv_b[i]), row(hy_bias_d[i])
        shared = (mcw, mcb, wq_bd, wk_bd, gate_b, hcw, hcb)
        (q_x, k_x, v_x), feat_x, og_x, (x0_x, z_x) = _mixer_in(
            x, g1, mx[1], mx[0], w_in_l, gq, gkv, wq, wqs, wk, wv, cos_x, sin_x, *shared, tmx)
        (q_c, k_c, v_c), feat_c, og_c, (x0_c, z_c) = _mixer_in(
            ctx, g1, mc[1], mc[0], w_in_l, gq, gkv, wq, wqs, wk, wv, cos_c, sin_c, *shared, tmc)

        att_x = _attention(q_x, k_c, v_c, k_x, v_x, min(1024, seq), min(256, seq))
        ml_x, ml_c = _ml_scan(feat_c, og_c, feat_x, og_x, row(ml_norm_g[i]), head_mean)
        hy_x = _hyena(x0_x, z_x, filt_w, hbias, tmx, min(512, seq), 2)

        x = _out_ffn(x, att_x, ml_x, hy_x, w_out_l, mx[2], row(norm2_g[i]), mx[4], mx[3], mx[5],
                     wup, fcw, fcb, wdn, row(final_norm_g), tmx, last)

        if not last:
            att_c = _attention(q_c, k_c, v_c, None, None, tmc, tmc)
            hy_c = _hyena(x0_c, z_c, filt_w, hbias, tmc, tmc, 2)
            ctx = _out_ffn(ctx, att_c, ml_c, hy_c, w_out_l, mc[2], row(norm2_g[i]), mc[4], mc[3],
                           mc[5], wup, fcw, fcb, wdn, row(final_norm_g), tmc, False)
    return x
```
